```python
import math
import jax, jax.numpy as jnp
from jax import lax
import numpy as np

D_MODEL = 1024
BATCH = 8
SEQ = 2048
DEPTH = 2
DEC_BATCH = 32
DEC_SEQ = 1
PAST_LEN = 16384
PAGE_SIZE = 128

N_A_LAYERS = DEPTH // 2
N_B_LAYERS = DEPTH - N_A_LAYERS
A_INNER = 2 * D_MODEL
A_HEADS = 4
A_HD = A_INNER // A_HEADS
A_CONV = 4
A_QKV_BLOCK = 4
A_CHUNK = 64
B_HEADS = 8
B_HD = D_MODEL // (2 * B_HEADS)
B_VD = 2 * B_HD
Q_BLOCK = 128
MOE_GROUPS = 4
MOE_EXPERTS = 4
MOE_TOPK = 2
MOE_FF = 512
DN_ALPHA = (2.0 * DEPTH) ** 0.25
DN_BETA = (8.0 * DEPTH) ** -0.25
LN_EPS = 1e-5

kernel_name = 'yoco_mlstm_diffattn_hmoe_step'


def _layer_norm(x, g, b):
    xf = x.astype(jnp.float32)
    xc = xf - jnp.mean(xf, axis=-1, keepdims=True)
    var = jnp.mean(xc * xc, axis=-1, keepdims=True)
    return (xc * lax.rsqrt(var + LN_EPS) * g.astype(jnp.float32) + b.astype(jnp.float32)).astype(x.dtype)


def _alibi_slopes():
    return jnp.exp2(-8.0 * jnp.arange(1, B_HEADS + 1, dtype=jnp.float32) / B_HEADS)


def _mlstm_chunk(carry, inp):
    c_st, n_st, m_st = carry
    q, k, v, ig, lf = inp
    L = q.shape[2]
    b = jnp.cumsum(lf, axis=-1)
    causal = jnp.tril(jnp.ones((L, L), dtype=bool))
    log_d = jnp.where(causal, b[..., :, None] - b[..., None, :] + ig[..., None, :], -jnp.inf)
    log_inter = b + m_st[..., None]
    m_t = jnp.maximum(log_inter, jnp.max(log_d, axis=-1))
    d = jnp.exp(log_d - m_t[..., None])
    w_inter = jnp.exp(log_inter - m_t)
    s = jnp.einsum('bhtd,bhsd->bhts', q, k) * d
    num = w_inter[..., None] * jnp.einsum('bhtd,bhde->bhte', q, c_st) + jnp.einsum('bhts,bhse->bhte', s, v)
    den = w_inter * jnp.einsum('bhtd,bhd->bht', q, n_st) + jnp.sum(s, axis=-1)
    h = num / jnp.maximum(jnp.abs(den), jnp.exp(-m_t))[..., None]
    m_new = m_t[..., -1]
    w_s = jnp.exp(b[..., -1:] - b + ig - m_new[..., None])
    decay = jnp.exp(b[..., -1] + m_st - m_new)
    c_new = decay[..., None, None] * c_st + jnp.einsum('bhs,bhsd,bhse->bhde', w_s, k, v)
    n_new = decay[..., None] * n_st + jnp.einsum('bhs,bhsd->bhd', w_s, k)
    return (c_new, n_new, m_new), h


def _mlstm_mixer(x, conv_buf, c0, n0, m0, chunk, w_in, conv_w, conv_b, w_q, w_k, w_v,
                 w_gate, b_gate, norm_w, skip, w_out):
    bsz, seq, _ = x.shape
    xm, z = jnp.split(x @ w_in, 2, axis=-1)
    xpad = jnp.concatenate([conv_buf.astype(xm.dtype), xm], axis=1)
    conv = conv_b
    for j in range(A_CONV):
        conv = conv + xpad[:, j:j + seq] * conv_w[j]
    xc = jax.nn.silu(conv)

    def headwise(u, w):
        u = jnp.einsum('bsnc,ncd->bsnd', u.reshape(bsz, seq, A_INNER // A_QKV_BLOCK, A_QKV_BLOCK), w)
        return u.reshape(bsz, seq, A_HEADS, A_HD)

    q = headwise(xc, w_q)
    k = headwise(xc, w_k) * (A_HD ** -0.5)
    v = headwise(xm, w_v)
    gates = (jnp.concatenate([q, k, v], axis=-1).reshape(bsz, seq, 3 * A_INNER) @ w_gate + b_gate).astype(jnp.float32)
    ig = gates[..., :A_HEADS]
    lf = jax.nn.log_sigmoid(gates[..., A_HEADS:])
    nc = seq // chunk

    def to_chunks(u):
        u = u.astype(jnp.float32).reshape((bsz, nc, chunk) + u.shape[2:])
        return jnp.moveaxis(u, (1, 3), (0, 2))

    carry0 = (c0.astype(jnp.float32), n0.astype(jnp.float32), m0.astype(jnp.float32))
    (c_new, n_new, m_new), hs = lax.scan(
        _mlstm_chunk, carry0, (to_chunks(q), to_chunks(k), to_chunks(v), to_chunks(ig), to_chunks(lf)))
    h = jnp.moveaxis(hs, (0, 2), (1, 3)).reshape(bsz, seq, A_HEADS, A_HD)
    hc = h - jnp.mean(h, axis=-1, keepdims=True)
    hn = hc * lax.rsqrt(jnp.mean(hc * hc, axis=-1, keepdims=True) + LN_EPS) * \
        norm_w.astype(jnp.float32).reshape(A_HEADS, A_HD)
    out = (hn.reshape(bsz, seq, A_INNER) + skip.astype(jnp.float32) * xc.astype(jnp.float32)) * \
        jax.nn.sigmoid(z.astype(jnp.float32))
    return out.astype(x.dtype) @ w_out, xpad[:, seq:], c_new, n_new, m_new


def _diff_lambda(lam_p, layer_idx):
    lam_init = 0.8 - 0.6 * math.exp(-0.3 * layer_idx)
    lp = lam_p.astype(jnp.float32)
    lam = jnp.exp(jnp.sum(lp[0] * lp[1])) - jnp.exp(jnp.sum(lp[2] * lp[3])) + lam_init
    return lam, lam_init


def _diff_attn_prompt(q, k, v, lam):
    bsz, seq = q.shape[:2]
    nb = seq // Q_BLOCK
    slopes = _alibi_slopes()
    kpos = jnp.arange(seq)
    vf = v.astype(jnp.float32)
    qb = jnp.moveaxis(q.reshape(bsz, nb, Q_BLOCK, B_HEADS, 2, B_HD), 1, 0)

    def block(args):
        qblk, start = args
        s = jnp.einsum('bqhcd,bkhcd->bhcqk', qblk, k).astype(jnp.float32)
        dist = (start + jnp.arange(Q_BLOCK))[:, None] - kpos[None, :]
        bias = jnp.where(dist >= 0, -slopes[:, None, None] * dist, -jnp.inf)
        p = jax.nn.softmax(s + bias[:, None], axis=-1)
        w = p[:, :, 0] - lam * p[:, :, 1]
        return jnp.einsum('bhqk,bkhe->bqhe', w, vf)

    out = lax.map(block, (qb, jnp.arange(nb) * Q_BLOCK))
    return jnp.moveaxis(out, 0, 1).reshape(bsz, seq, B_HEADS, B_VD)


def _diff_attn_sample(q, k_new, v_new, cache_k, cache_v, page_table, lam):
    dbsz, dseq = q.shape[:2]
    n_pages = page_table.shape[1]
    past = n_pages * PAGE_SIZE
    slopes = _alibi_slopes()

    def page_scores(phys):
        kp = cache_k[phys].reshape(dbsz, PAGE_SIZE, B_HEADS, 2, B_HD).astype(q.dtype)
        return jnp.einsum('bqhcd,bkhcd->bhcqk', q, kp).astype(jnp.float32)

    s_past = lax.map(page_scores, page_table.T)
    s_past = jnp.moveaxis(s_past, 0, 4).reshape(dbsz, B_HEADS, 2, dseq, past)
    qpos = past + jnp.arange(dseq)
    bias_past = -slopes[:, None, None] * (qpos[:, None] - jnp.arange(past)[None, :])
    d_self = qpos[:, None] - qpos[None, :]
    bias_self = jnp.where(d_self >= 0, -slopes[:, None, None] * d_self, -jnp.inf)
    s_self = jnp.einsum('bqhcd,bkhcd->bhcqk', q, k_new).astype(jnp.float32)
    s = jnp.concatenate([s_past + bias_past[:, None], s_self + bias_self[:, None]], axis=-1)
    p = jax.nn.softmax(s, axis=-1)
    w = p[:, :, 0] - lam * p[:, :, 1]
    w_past = jnp.moveaxis(w[..., :past].reshape(dbsz, B_HEADS, dseq, n_pages, PAGE_SIZE), 3, 0)

    def page_values(acc, args):
        phys, wp = args
        vp = cache_v[phys].astype(jnp.float32)
        return acc + jnp.einsum('bhqk,bkhe->bqhe', wp, vp), None

    acc0 = jnp.einsum('bhqk,bkhe->bqhe', w[..., past:], v_new.astype(jnp.float32))
    out, _ = lax.scan(page_values, acc0, (page_table.T, w_past))
    return out


def _hier_moe(x, w_group, b_group, w_expert, b_expert, w1, w3, w2):
    bsz, seq, d = x.shape
    t = x.reshape(-1, d)
    g_logits = (t @ w_group + b_group).astype(jnp.float32)
    g_prob = jax.nn.softmax(g_logits, axis=-1)
    _, g_idx = lax.top_k(g_logits, 1)
    g_w = jnp.take_along_axis(g_prob, g_idx, axis=-1)
    e_logits = (jnp.einsum('td,gde->tge', t, w_expert) + b_expert).astype(jnp.float32)
    e_sel = jnp.take_along_axis(e_logits, g_idx[:, :, None], axis=1)[:, 0]
    top_v, top_i = lax.top_k(e_sel, MOE_TOPK)
    top_w = jax.nn.softmax(top_v, axis=-1) * g_w
    e_w = jnp.sum(jax.nn.one_hot(top_i, MOE_EXPERTS, dtype=jnp.float32) * top_w[..., None], axis=1)
    comb = (jax.nn.one_hot(g_idx[:, 0], MOE_GROUPS, dtype=jnp.float32)[:, :, None] * e_w[:, None, :]).astype(x.dtype)
    y = jnp.zeros_like(t)
    for g in range(MOE_GROUPS):
        h = jax.nn.silu(jnp.einsum('td,edf->tef', t, w1[g])) * jnp.einsum('td,edf->tef', t, w3[g])
        y = y + jnp.einsum('tef,efd->td', h * comb[:, g, :, None], w2[g])
    return y.reshape(bsz, seq, d)


def _trunk(x, c, conv_buf, st_c, st_n, st_m, chunk, cache_k, cache_v, page_table, weights):
    (a_w_in, a_conv_w, a_conv_b, a_w_q, a_w_k, a_w_v, a_w_gate, a_b_gate, a_norm_w, a_skip, a_w_out,
     kv_w_k, kv_w_v, b_w_q, b_lambda, b_subln_w, b_w_out,
     moe_w_group, moe_b_group, moe_w_expert, moe_b_expert, moe_w1, moe_w3, moe_w2,
     ln_g, ln_b, ada_w, ada_b) = weights
    bsz, seq, _ = x.shape
    new_conv, new_c, new_n, new_m = [], [], [], []
    k_sh = None
    v_sh = None
    for i in range(DEPTH):
        mod = (jax.nn.silu(c) @ ada_w[i] + ada_b[i])[:, None, :]
        sh1, sc1, g1, sh2, sc2, g2 = jnp.split(mod, 6, axis=-1)
        h = x * (1 + sc1) + sh1
        if i < N_A_LAYERS:
            out, cb, cn, nn_, mn = _mlstm_mixer(
                h, conv_buf[i], st_c[i], st_n[i], st_m[i], chunk, a_w_in[i], a_conv_w[i], a_conv_b[i],
                a_w_q[i], a_w_k[i], a_w_v[i], a_w_gate[i], a_b_gate[i], a_norm_w[i], a_skip[i], a_w_out[i])
            new_conv.append(cb)
            new_c.append(cn)
            new_n.append(nn_)
            new_m.append(mn)
        else:
            j = i - N_A_LAYERS
            lam, lam_init = _diff_lambda(b_lambda[j], i)
            q = (h @ b_w_q[j]).reshape(bsz, seq, B_HEADS, 2, B_HD) * (B_HD ** -0.5)
            k5 = k_sh.reshape(bsz, seq, B_HEADS, 2, B_HD)
            if page_table is None:
                o = _diff_attn_prompt(q, k5, v_sh, lam)
            else:
                o = _diff_attn_sample(q, k5, v_sh, cache_k, cache_v, page_table, lam)
            o = o * lax.rsqrt(jnp.mean(o * o, axis=-1, keepdims=True) + LN_EPS) * \
                b_subln_w[j].astype(jnp.float32) * (1.0 - lam_init)
            out = o.reshape(bsz, seq, B_HEADS * B_VD).astype(x.dtype) @ b_w_out[j]
        x = _layer_norm(DN_ALPHA * x + (1 + g1) * out, ln_g[i, 0], ln_b[i, 0])
        h = x * (1 + sc2) + sh2
        ffn = _hier_moe(h, moe_w_group[i], moe_b_group[i], moe_w_expert[i], moe_b_expert[i],
                        moe_w1[i], moe_w3[i], moe_w2[i])
        x = _layer_norm(DN_ALPHA * x + (1 + g2) * ffn, ln_g[i, 1], ln_b[i, 1])
        if i == N_A_LAYERS - 1:
            k_sh = (x @ kv_w_k).reshape(bsz, seq, B_HEADS, B_VD)
            v_sh = (x @ kv_w_v).reshape(bsz, seq, B_HEADS, B_VD)
    return (x, k_sh, v_sh, jnp.stack(new_conv), jnp.stack(new_c), jnp.stack(new_n), jnp.stack(new_m))


def setup_inputs(seed: int = 0) -> dict:
    key = jax.random.key(seed)
    ks = iter(jax.random.split(key, 48))
    f32 = jnp.float32

    def nrm(shape, scale):
        return jax.random.normal(next(ks), shape, f32) * scale

    n_pages = PAST_LEN // PAGE_SIZE
    n_used = DEC_BATCH * n_pages
    n_pool = (n_used * 5) // 4
    perm = jax.random.permutation(next(ks), n_pool)
    page_table = perm[:n_used].reshape(DEC_BATCH, n_pages).astype(jnp.int32)
    kv_dim = B_HEADS * B_VD
    f_bias = jnp.linspace(3.0, 6.0, A_HEADS, dtype=f32)
    a_b_gate = jnp.concatenate([nrm((N_A_LAYERS, A_HEADS), 0.1),
                                f_bias[None, :] + nrm((N_A_LAYERS, A_HEADS), 0.1)], axis=-1)
    return {
        'x_prompt': nrm((BATCH, SEQ, D_MODEL), 1.0),
        'x_sample': nrm((DEC_BATCH, DEC_SEQ, D_MODEL), 1.0),
        'cache_k': nrm((n_pool, PAGE_SIZE, B_HEADS, B_VD), 1.0),
        'cache_v': nrm((n_pool, PAGE_SIZE, B_HEADS, B_VD), 1.0),
        'state_conv': nrm((N_A_LAYERS, DEC_BATCH, A_CONV - 1, A_INNER), 1.0),
        'state_C': nrm((N_A_LAYERS, DEC_BATCH, A_HEADS, A_HD, A_HD), 0.05),
        'state_n': nrm((N_A_LAYERS, DEC_BATCH, A_HEADS, A_HD), 0.1),
        'state_m': nrm((N_A_LAYERS, DEC_BATCH, A_HEADS), 1.0),
        'page_table': page_table,
        'c_prompt': nrm((BATCH, D_MODEL), 1.0),
        'c_sample': nrm((DEC_BATCH, D_MODEL), 1.0),
        'a_w_in': nrm((N_A_LAYERS, D_MODEL, 2 * A_INNER), D_MODEL ** -0.5),
        'a_conv_w': nrm((N_A_LAYERS, A_CONV, A_INNER), A_CONV ** -0.5),
        'a_conv_b': nrm((N_A_LAYERS, A_INNER), 0.02),
        'a_w_q': nrm((N_A_LAYERS, A_INNER // A_QKV_BLOCK, A_QKV_BLOCK, A_QKV_BLOCK), A_QKV_BLOCK ** -0.5),
        'a_w_k': nrm((N_A_LAYERS, A_INNER // A_QKV_BLOCK, A_QKV_BLOCK, A_QKV_BLOCK), A_QKV_BLOCK ** -0.5),
        'a_w_v': nrm((N_A_LAYERS, A_INNER // A_QKV_BLOCK, A_QKV_BLOCK, A_QKV_BLOCK), A_QKV_BLOCK ** -0.5),
        'a_w_gate': nrm((N_A_LAYERS, 3 * A_INNER, 2 * A_HEADS), (3 * A_INNER) ** -0.5),
        'a_b_gate': a_b_gate,
        'a_norm_w': 1.0 + nrm((N_A_LAYERS, A_INNER), 0.02),
        'a_skip': 1.0 + nrm((N_A_LAYERS, A_INNER), 0.02),
        'a_w_out': nrm((N_A_LAYERS, A_INNER, D_MODEL), A_INNER ** -0.5 * DN_BETA),
        'kv_w_k': nrm((D_MODEL, kv_dim), D_MODEL ** -0.5),
        'kv_w_v': nrm((D_MODEL, kv_dim), D_MODEL ** -0.5),
        'b_w_q': nrm((N_B_LAYERS, D_MODEL, kv_dim), D_MODEL ** -0.5),
        'b_lambda': nrm((N_B_LAYERS, 4, B_HD), 0.1),
        'b_subln_w': 1.0 + nrm((N_B_LAYERS, B_VD), 0.02),
        'b_w_out': nrm((N_B_LAYERS, kv_dim, D_MODEL), kv_dim ** -0.5 * DN_BETA),
        'moe_w_group': nrm((DEPTH, D_MODEL, MOE_GROUPS), D_MODEL ** -0.5),
        'moe_b_group': nrm((DEPTH, MOE_GROUPS), 0.01),
        'moe_w_expert': nrm((DEPTH, MOE_GROUPS, D_MODEL, MOE_EXPERTS), D_MODEL ** -0.5),
        'moe_b_expert': nrm((DEPTH, MOE_GROUPS, MOE_EXPERTS), 0.01),
        'moe_w1': nrm((DEPTH, MOE_GROUPS, MOE_EXPERTS, D_MODEL, MOE_FF), D_MODEL ** -0.5),
        'moe_w3': nrm((DEPTH, MOE_GROUPS, MOE_EXPERTS, D_MODEL, MOE_FF), D_MODEL ** -0.5),
        'moe_w2': nrm((DEPTH, MOE_GROUPS, MOE_EXPERTS, MOE_FF, D_MODEL), MOE_FF ** -0.5 * DN_BETA),
        'ln_g': 1.0 + nrm((DEPTH, 2, D_MODEL), 0.02),
        'ln_b': nrm((DEPTH, 2, D_MODEL), 0.02),
        'ada_w': nrm((DEPTH, D_MODEL, 6 * D_MODEL), 0.3 * D_MODEL ** -0.5),
        'ada_b': nrm((DEPTH, 6 * D_MODEL), 0.02),
    }


def reference(x_prompt, x_sample, cache_k, cache_v, state_conv, state_C, state_n, state_m, page_table,
              c_prompt, c_sample, a_w_in, a_conv_w, a_conv_b, a_w_q, a_w_k, a_w_v, a_w_gate, a_b_gate,
              a_norm_w, a_skip, a_w_out, kv_w_k, kv_w_v, b_w_q, b_lambda, b_subln_w, b_w_out,
              moe_w_group, moe_b_group, moe_w_expert, moe_b_expert, moe_w1, moe_w3, moe_w2,
              ln_g, ln_b, ada_w, ada_b):
    weights = (a_w_in, a_conv_w, a_conv_b, a_w_q, a_w_k, a_w_v, a_w_gate, a_b_gate, a_norm_w, a_skip, a_w_out,
               kv_w_k, kv_w_v, b_w_q, b_lambda, b_subln_w, b_w_out,
               moe_w_group, moe_b_group, moe_w_expert, moe_b_expert, moe_w1, moe_w3, moe_w2,
               ln_g, ln_b, ada_w, ada_b)
    bp = x_prompt.shape[0]
    zero_conv = jnp.zeros((N_A_LAYERS, bp, A_CONV - 1, A_INNER), x_prompt.dtype)
    zero_c = jnp.zeros((N_A_LAYERS, bp, A_HEADS, A_HD, A_HD), jnp.float32)
    zero_n = jnp.zeros((N_A_LAYERS, bp, A_HEADS, A_HD), jnp.float32)
    zero_m = jnp.zeros((N_A_LAYERS, bp, A_HEADS), jnp.float32)
    y_prompt, k_prompt, v_prompt, conv_prompt, C_prompt, n_prompt, m_prompt = _trunk(
        x_prompt, c_prompt, zero_conv, zero_c, zero_n, zero_m, A_CHUNK, None, None, None, weights)
    y_sample, k_sample, v_sample, conv_sample, C_sample, n_sample, m_sample = _trunk(
        x_sample, c_sample, state_conv, state_C, state_n, state_m, x_sample.shape[1],
        cache_k, cache_v, page_table, weights)
    return (y_prompt, y_sample, k_prompt, v_prompt, conv_prompt, C_prompt, n_prompt, m_prompt,
            k_sample, v_sample, conv_sample, C_sample, n_sample, m_sample)
```

```python
import functools
import math

import jax
import jax.numpy as jnp
from jax import lax
from jax.experimental import pallas as pl
from jax.experimental.pallas import tpu as pltpu

F32 = jnp.float32
BF16 = jnp.bfloat16
I32 = jnp.int32
U32 = jnp.uint32

DEPTH = 2
LN_EPS = 1e-5
DN_ALPHA = (2.0 * DEPTH) ** 0.25
A_CONV = 4
A_QKV_BLOCK = 4
MOE_GROUPS = 4
MOE_EXPERTS = 4
N_EXPERTS = MOE_GROUPS * MOE_EXPERTS
PAGE_SIZE = 128
LANES = 128
PAGES_PER_STEP = 8
MIB = 1024 * 1024


def _cparams(sem, vmem_mib=48):
    return pltpu.CompilerParams(dimension_semantics=sem, vmem_limit_bytes=vmem_mib * MIB)


def _silu(x):
    return x / (1.0 + jnp.exp(-x))


def _log_sigmoid(x):
    return jnp.minimum(x, 0.0) - jnp.log(1.0 + jnp.exp(-jnp.abs(x)))


def _dot(a, b):
    return jnp.dot(a, b, preferred_element_type=F32)


def _dot_nt(a, b):
    return lax.dot_general(a, b, (((1,), (1,)), ((), ())), preferred_element_type=F32)


def _dot_tn(a, b):
    return lax.dot_general(a, b, (((0,), (0,)), ((), ())), preferred_element_type=F32)


def _bits(x):
    return lax.bitcast_convert_type(x, U32)


def _pack_pairs(y):
    w = y.shape[-1] // 2
    r = _bits(y.astype(BF16).astype(F32))
    return (r[:, w:] & jnp.uint32(0xFFFF0000)) | (r[:, :w] >> 16)


def _unpack_pairs(u):
    lo = lax.bitcast_convert_type(u << 16, F32)
    hi = lax.bitcast_convert_type(u & jnp.uint32(0xFFFF0000), F32)
    return lo, hi


def _layer_norm(y, g, b):
    mu = jnp.mean(y, axis=-1, keepdims=True)
    yc = y - mu
    var = jnp.mean(yc * yc, axis=-1, keepdims=True)
    return yc * lax.rsqrt(var + LN_EPS) * g + b


def _ada_body(c_ref, w_ref, b_ref, o_ref):
    s = _silu(c_ref[...])
    o_ref[0] = _dot(s.astype(BF16), w_ref[0].astype(BF16)) + b_ref[0]


def _ada(c_all, ada_w, ada_b):
    r, d = c_all.shape
    depth, _, n = ada_w.shape
    tn = 1536
    return pl.pallas_call(
        _ada_body,
        grid=(depth, n // tn),
        in_specs=[pl.BlockSpec((r, d), lambda i, j: (0, 0)),
                  pl.BlockSpec((1, d, tn), lambda i, j: (i, 0, j)),
                  pl.BlockSpec((1, 1, tn), lambda i, j: (i, 0, j))],
        out_specs=pl.BlockSpec((1, r, tn), lambda i, j: (i, 0, j)),
        out_shape=jax.ShapeDtypeStruct((depth, r, n), F32),
        compiler_params=_cparams(("parallel", "parallel")),
        name="ada",
    )(c_all, ada_w, ada_b.reshape(depth, 1, n))


def _mod_spec(mod, ts, col):
    d = mod.shape[-1] // 6
    if mod.shape[1] == 1:
        return pl.BlockSpec((1, 1, d), lambda b, s, *_: (b, 0, col))
    return pl.BlockSpec((1, ts, d), lambda b, s, *_: (b, s, col))


def _lin_body(*refs, has_mod, out_scale):
    if has_mod:
        x_ref, sc_ref, sh_ref, w_ref, *o_refs = refs
        x = x_ref[0] * (1.0 + sc_ref[0]) + sh_ref[0]
    else:
        x_ref, w_ref, *o_refs = refs
        x = x_ref[0]
    y = _dot(x.astype(BF16), w_ref[...])
    if out_scale != 1.0:
        y = y * out_scale
    n = y.shape[-1] // len(o_refs)
    for j, o_ref in enumerate(o_refs):
        o_ref[0] = y[:, j * n:(j + 1) * n].astype(o_ref.dtype)


def _linear(x, w, ts, out_dtypes, mod=None, cols=None, out_scale=1.0, name="linear"):
    b, s, k = x.shape
    n = w.shape[1]
    no = n // len(out_dtypes)
    in_specs = [pl.BlockSpec((1, ts, k), lambda i, j: (i, j, 0))]
    args = [x]
    if mod is not None:
        in_specs += [_mod_spec(mod, ts, cols[0]), _mod_spec(mod, ts, cols[1])]
        args += [mod, mod]
    in_specs.append(pl.BlockSpec((k, n), lambda i, j: (0, 0)))
    args.append(w)
    return pl.pallas_call(
        functools.partial(_lin_body, has_mod=mod is not None, out_scale=out_scale),
        grid=(b, s // ts),
        in_specs=in_specs,
        out_specs=[pl.BlockSpec((1, ts, no), lambda i, j: (i, j, 0)) for _ in out_dtypes],
        out_shape=[jax.ShapeDtypeStruct((b, s, no), dt) for dt in out_dtypes],
        compiler_params=_cparams(("parallel", "parallel"), 56),
        name=name,
    )(*args)


def _headwise(u, c_ref):
    n = u.shape[-1]
    out = u * c_ref[3:4, :]
    for delta in (-3, -2, -1, 1, 2, 3):
        out = out + pltpu.roll(u, (-delta) % n, 1) * c_ref[delta + 3:delta + 4, :]
    return out


def _p1_body(xm_ref, tail_ref, cw_ref, cb_ref, cq_ref, ck_ref, cv_ref, wg_ref, wgt_ref, bg_ref, bgt_ref,
             xc_ref, q_ref, k_ref, v_ref, g_ref, *rest, want_gt):
    if want_gt:
        gt_ref, conv_ref, xbuf = rest
    else:
        conv_ref, xbuf = rest
    s = pl.program_id(1)
    ts = xm_ref.shape[1]
    inner = xm_ref.shape[2]

    @pl.when(s == 0)
    def _():
        xbuf[0:8, :] = tail_ref[0]

    if ts >= 8:
        @pl.when(s > 0)
        def _():
            xbuf[0:8, :] = xbuf[ts:ts + 8, :]

    xm = xm_ref[0]
    xbuf[8:8 + ts, :] = xm
    conv = cb_ref[...] + xm * cw_ref[A_CONV - 1:A_CONV, :]
    for j in range(A_CONV - 1):
        conv = conv + xbuf[5 + j:5 + j + ts, :] * cw_ref[j:j + 1, :]
    conv_ref[0] = xbuf[ts:ts + 8, :]
    xc = _silu(conv)
    xc_ref[0] = xc
    q = _headwise(xc, cq_ref).astype(BF16)
    k = _headwise(xc, ck_ref).astype(BF16)
    v = _headwise(xm, cv_ref).astype(BF16)
    q_ref[0] = q
    k_ref[0] = k
    v_ref[0] = v
    g = (_dot(q, wg_ref[0:inner, :]) + _dot(k, wg_ref[inner:2 * inner, :])
         + _dot(v, wg_ref[2 * inner:3 * inner, :]) + bg_ref[...])
    half = wgt_ref.shape[0] // 2
    lane = lax.broadcasted_iota(I32, g.shape, 1)
    g_ref[0] = jnp.where(lane >= half, _log_sigmoid(g), g)[:, 0:8]
    if want_gt:
        gt = (_dot_nt(wgt_ref[:, 0:inner], q) + _dot_nt(wgt_ref[:, inner:2 * inner], k)
              + _dot_nt(wgt_ref[:, 2 * inner:3 * inner], v) + bgt_ref[...])
        sub = lax.broadcasted_iota(I32, gt.shape, 0)
        gt_ref[0] = jnp.where(sub >= half, _log_sigmoid(gt), gt)


def _mlstm_pre(xm, tail8, prep, ts, want_gt):
    b, s, inner = xm.shape
    assert s == ts or ts >= 8
    full = lambda shape: pl.BlockSpec(shape, lambda i, j: (0,) * len(shape))
    row = lambda n, dt: (pl.BlockSpec((1, ts, n), lambda i, j: (i, j, 0)), jax.ShapeDtypeStruct((b, s, n), dt))
    outs = [row(inner, F32), row(inner, BF16), row(inner, BF16), row(inner, BF16), row(8, F32)]
    if want_gt:
        outs.append((pl.BlockSpec((1, 8, ts), lambda i, j: (i, 0, j)), jax.ShapeDtypeStruct((b, 8, s), F32)))
    outs.append((pl.BlockSpec((1, 8, inner), lambda i, j: (i, 0, 0)), jax.ShapeDtypeStruct((b, 8, inner), F32)))
    return pl.pallas_call(
        functools.partial(_p1_body, want_gt=want_gt),
        grid=(b, s // ts),
        in_specs=[pl.BlockSpec((1, ts, inner), lambda i, j: (i, j, 0)),
                  pl.BlockSpec((1, 8, inner), lambda i, j: (i, 0, 0)),
                  full((A_CONV, inner)), full((1, inner)),
                  full((8, inner)), full((8, inner)), full((8, inner)),
                  full((3 * inner, LANES)), full((8, 3 * inner)), full((1, LANES)), full((8, 1))],
        out_specs=[o[0] for o in outs],
        out_shape=[o[1] for o in outs],
        scratch_shapes=[pltpu.VMEM((ts + 8, inner), F32)],
        compiler_params=_cparams(("parallel", "arbitrary"), 56),
        name="mlstm_pre",
    )(xm, tail8, prep["conv_w"], prep["conv_b"], prep["cq"], prep["ck"], prep["cv"],
      prep["wg"], prep["wgt"], prep["bg"], prep["bgt"])


def _p2_body(q_ref, k_ref, v_ref, gc_ref, gr_ref, h_ref, c_ref, n_ref, m_ref, caug, ms):
    c = pl.program_id(2)
    nc = pl.num_programs(2)
    L = q_ref.shape[1]
    hd = q_ref.shape[2]

    @pl.when(c == 0)
    def _():
        caug[...] = jnp.zeros_like(caug)
        ms[...] = jnp.zeros_like(ms)

    q = q_ref[0]
    k = k_ref[0]
    v = v_ref[0]
    gc = gc_ref[0, 0]
    gr = gr_ref[0, 0]
    ig_c, lf_c = gc[:, 0:1], gc[:, 1:2]
    ig_r, lf_r = gr[0:1, :], gr[1:2, :]
    row = lax.broadcasted_iota(I32, (L, L), 0)
    col = lax.broadcasted_iota(I32, (L, L), 1)
    causal = col <= row
    b_c = jnp.sum(jnp.where(causal, lf_r, 0.0), axis=1, keepdims=True)
    b_r = jnp.sum(jnp.where(row <= col, lf_c, 0.0), axis=0, keepdims=True)
    log_d = jnp.where(causal, b_c - b_r + ig_r, -jnp.inf)
    m_prev = ms[...]
    log_inter = b_c + m_prev
    m_t = jnp.maximum(log_inter, jnp.max(log_d, axis=1, keepdims=True))
    d = jnp.exp(log_d - m_t)
    w_inter = jnp.exp(log_inter - m_t)
    s = (_dot_nt(q, k) * d).astype(BF16)
    lane = lax.broadcasted_iota(I32, (L, LANES), 1)
    vaug = jnp.concatenate([v, jnp.where(lane == 0, 1.0, 0.0).astype(BF16)], axis=1)
    ca = caug[...]
    num = w_inter * _dot(q, ca.astype(BF16)) + _dot(s, vaug)
    den = num[:, hd:hd + 1]
    h_ref[0] = num[:, 0:hd] / jnp.maximum(jnp.abs(den), jnp.exp(-m_t))
    m_new = m_t[L - 1:L, :]
    b_last = b_c[L - 1:L, :]
    w_s = jnp.exp(b_last - b_c + ig_c - m_new)
    decay = jnp.exp(b_last + m_prev - m_new)
    kw = (k.astype(F32) * w_s).astype(BF16)
    caug[...] = decay * ca + _dot_tn(kw, vaug)
    ms[...] = m_new

    @pl.when(c == nc - 1)
    def _():
        c_ref[0, 0] = caug[:, 0:hd]
        n_ref[0, 0] = caug[:, hd:hd + 1]
        m_ref[0, 0] = m_new


def _mlstm_scan(q, k, v, gcol, grow, L):
    b, s, inner = q.shape
    heads = gcol.shape[1]
    hd = inner // heads
    qkv = pl.BlockSpec((1, L, hd), lambda i, h, c: (i, c, h))
    return pl.pallas_call(
        _p2_body,
        grid=(b, heads, s // L),
        in_specs=[qkv, qkv, qkv,
                  pl.BlockSpec((1, 1, L, 2), lambda i, h, c: (i, h, c, 0)),
                  pl.BlockSpec((1, 1, 2, L), lambda i, h, c: (i, h, 0, c))],
        out_specs=[qkv,
                   pl.BlockSpec((1, 1, hd, hd), lambda i, h, c: (i, h, 0, 0)),
                   pl.BlockSpec((1, 1, hd, 1), lambda i, h, c: (i, h, 0, 0)),
                   pl.BlockSpec((1, 1, 1, 1), lambda i, h, c: (i, h, 0, 0))],
        out_shape=[jax.ShapeDtypeStruct((b, s, inner), F32),
                   jax.ShapeDtypeStruct((b, heads, hd, hd), F32),
                   jax.ShapeDtypeStruct((b, heads, hd, 1), F32),
                   jax.ShapeDtypeStruct((b, heads, 1, 1), F32)],
        scratch_shapes=[pltpu.VMEM((hd, hd + LANES), F32), pltpu.VMEM((1, 1), F32)],
        compiler_params=_cparams(("parallel", "parallel", "arbitrary")),
        name="mlstm_scan",
    )(q, k, v, gcol, grow)


def _s2_body(q_ref, k_ref, v_ref, g_ref, c_ref, n_ref, m_ref, h_ref, co_ref, no_ref, mo_ref):
    q = q_ref[0]
    k = k_ref[0]
    v = v_ref[0]
    g = g_ref[0, 0]
    ig, lf = g[:, 0:1], g[:, 1:2]
    m_prev = m_ref[0, 0]
    m_new = jnp.maximum(lf + m_prev, ig)
    decay = jnp.exp(lf + m_prev - m_new)
    dd = jnp.exp(ig - m_new)
    hd = q.shape[1]
    sub = lax.broadcasted_iota(I32, (8, hd), 0)
    first = sub == 0
    k8 = jnp.where(first, k.astype(F32), 0.0).astype(BF16)
    v8 = jnp.where(first, v.astype(F32), 0.0).astype(BF16)
    q8 = jnp.where(first, q.astype(F32), 0.0).astype(BF16)
    c_new = decay * c_ref[0, 0] + dd * _dot_tn(k8, v8)
    n_new = decay * n_ref[0, 0] + dd * k.astype(F32)
    num = _dot(q8, c_new.astype(BF16))[0:1, :]
    den = jnp.sum(q.astype(F32) * n_new, axis=1, keepdims=True)
    h_ref[0] = num / jnp.maximum(jnp.abs(den), jnp.exp(-m_new))
    co_ref[0, 0] = c_new
    no_ref[0, 0] = n_new
    mo_ref[0, 0] = m_new


def _mlstm_step(q, k, v, g4, c0, n0, m0):
    b, _, inner = q.shape
    heads = c0.shape[1]
    hd = inner // heads
    qkv = pl.BlockSpec((1, 1, hd), lambda i, h: (i, 0, h))
    st = lambda r, c: pl.BlockSpec((1, 1, r, c), lambda i, h: (i, h, 0, 0))
    return pl.pallas_call(
        _s2_body,
        grid=(b, heads),
        in_specs=[qkv, qkv, qkv, st(1, 2), st(hd, hd), st(1, hd), st(1, 1)],
        out_specs=[qkv, st(hd, hd), st(1, hd), st(1, 1)],
        out_shape=[jax.ShapeDtypeStruct((b, 1, inner), F32),
                   jax.ShapeDtypeStruct((b, heads, hd, hd), F32),
                   jax.ShapeDtypeStruct((b, heads, 1, hd), F32),
                   jax.ShapeDtypeStruct((b, heads, 1, 1), F32)],
        compiler_params=_cparams(("parallel", "parallel")),
        name="mlstm_step",
    )(q, k, v, g4, c0, n0, m0)


def _route(logits):
    ts = logits.shape[0]
    lane = lax.broadcasted_iota(I32, (ts, LANES), 1)
    neg = -jnp.inf
    big = jnp.int32(1 << 20)
    is_g = lane < MOE_GROUPS
    gl = jnp.where(is_g, logits, neg)
    gmax = jnp.max(gl, axis=1, keepdims=True)
    gidx = jnp.min(jnp.where(gl == gmax, lane, big), axis=1, keepdims=True)
    gsum = jnp.sum(jnp.where(is_g, jnp.exp(logits - gmax), 0.0), axis=1, keepdims=True)
    g_w = 1.0 / gsum
    lo = MOE_GROUPS + MOE_EXPERTS * gidx
    el = jnp.where(lane >= lo, jnp.where(lane < lo + MOE_EXPERTS, logits, neg), neg)
    v1 = jnp.max(el, axis=1, keepdims=True)
    i1 = jnp.min(jnp.where(el == v1, lane, big), axis=1, keepdims=True)
    el2 = jnp.where(lane == i1, neg, el)
    v2 = jnp.max(el2, axis=1, keepdims=True)
    i2 = jnp.min(jnp.where(el2 == v2, lane, big), axis=1, keepdims=True)
    t = jnp.exp(v2 - v1)
    w1 = g_w / (1.0 + t)
    w2 = g_w * t / (1.0 + t)
    e1 = (i1 - MOE_GROUPS).astype(F32)
    e2 = (i2 - MOE_GROUPS).astype(F32)
    return jnp.where(lane == 0, e1, jnp.where(lane == 1, e2, jnp.where(lane == 2, w1, jnp.where(lane == 3, w2, 0.0))))


def _epilogue(x, o, g1, sc2, sh2, lng, lnb, wrh, wrl, br, x1_ref, h2_ref, ri_ref):
    x1 = _layer_norm(DN_ALPHA * x + (1.0 + g1) * o, lng, lnb)
    x1_ref[0] = x1
    h2 = x1 * (1.0 + sc2) + sh2
    h2_ref[0] = _pack_pairs(h2)
    xh = h2.astype(BF16)
    xl = (h2 - xh.astype(F32)).astype(BF16)
    logits = _dot(xh, wrh) + _dot(xl, wrh) + _dot(xh, wrl) + br
    ri_ref[0] = _route(logits)


def _mixa_body(h_ref, xc_ref, z_ref, x_ref, g1_ref, sc2_ref, sh2_ref, nw_ref, skip_ref, wo_ref,
               lng_ref, lnb_ref, wrh_ref, wrl_ref, br_ref, x1_ref, h2_ref, ri_ref, *, heads):
    h = h_ref[0]
    inner = h.shape[1]
    hd = inner // heads
    parts = []
    for a in range(heads):
        ha = h[:, a * hd:(a + 1) * hd]
        hc = ha - jnp.mean(ha, axis=1, keepdims=True)
        parts.append(hc * lax.rsqrt(jnp.mean(hc * hc, axis=1, keepdims=True) + LN_EPS))
    hn = jnp.concatenate(parts, axis=1) * nw_ref[...]
    z = z_ref[0]
    out = (hn + skip_ref[...] * xc_ref[0]) * (1.0 / (1.0 + jnp.exp(-z)))
    o = _dot(out.astype(BF16), wo_ref[...])
    _epilogue(x_ref[0], o, g1_ref[0], sc2_ref[0], sh2_ref[0], lng_ref[...], lnb_ref[...],
              wrh_ref[...], wrl_ref[...], br_ref[...], x1_ref, h2_ref, ri_ref)


def _mixb_body(o_ref, x_ref, g1_ref, sc2_ref, sh2_ref, sw_ref, wo_ref,
               lng_ref, lnb_ref, wrh_ref, wrl_ref, br_ref, x1_ref, h2_ref, ri_ref, *, heads, gain):
    o = o_ref[0]
    vd = o.shape[1] // heads
    parts = []
    for a in range(heads):
        oa = o[:, a * vd:(a + 1) * vd]
        parts.append(oa * lax.rsqrt(jnp.mean(oa * oa, axis=1, keepdims=True) + LN_EPS) * sw_ref[...] * gain)
    on = jnp.concatenate(parts, axis=1)
    y = _dot(on.astype(BF16), wo_ref[...])
    _epilogue(x_ref[0], y, g1_ref[0], sc2_ref[0], sh2_ref[0], lng_ref[...], lnb_ref[...],
              wrh_ref[...], wrl_ref[...], br_ref[...], x1_ref, h2_ref, ri_ref)


def _mixer_out(body, acts, x, mod, consts, ts, name):
    b, s, d = x.shape
    rowspec = lambda a: pl.BlockSpec((1, ts, a.shape[2]), lambda i, j: (i, j, 0))
    full = lambda a: pl.BlockSpec(a.shape, lambda i, j: (0,) * a.ndim)
    in_specs = [rowspec(a) for a in acts] + [rowspec(x)]
    in_specs += [_mod_spec(mod, ts, 2), _mod_spec(mod, ts, 4), _mod_spec(mod, ts, 3)]
    in_specs += [full(c) for c in consts]
    return pl.pallas_call(
        body,
        grid=(b, s // ts),
        in_specs=in_specs,
        out_specs=[pl.BlockSpec((1, ts, d), lambda i, j: (i, j, 0)),
                   pl.BlockSpec((1, ts, d // 2), lambda i, j: (i, j, 0)),
                   pl.BlockSpec((1, ts, LANES), lambda i, j: (i, j, 0))],
        out_shape=[jax.ShapeDtypeStruct((b, s, d), F32),
                   jax.ShapeDtypeStruct((b, s, d // 2), U32),
                   jax.ShapeDtypeStruct((b, s, LANES), F32)],
        compiler_params=_cparams(("parallel", "parallel"), 56),
        name=name,
    )(*acts, x, mod, mod, mod, *consts)


def _sort_body(e1_ref, e2_ref, pa_ref, pb_ref, cnt_ref, *, tmx, cw):
    nb_tok = e1_ref.shape[2]
    e1 = e1_ref[0]
    e2 = e2_ref[0]
    sub = lax.broadcasted_iota(I32, (N_EXPERTS, nb_tok), 0)
    a_hot = sub == e1
    b_hot = sub == e2
    m = jnp.where(a_hot, 1.0, 0.0) + jnp.where(b_hot, 1.0, 0.0)
    cnt = jnp.sum(m, axis=1, keepdims=True)
    padded = jnp.floor((cnt + (tmx - 1)) / tmx) * tmx
    r16 = lax.broadcasted_iota(I32, (N_EXPERTS, N_EXPERTS), 0)
    c16 = lax.broadcasted_iota(I32, (N_EXPERTS, N_EXPERTS), 1)
    prow = jnp.sum(jnp.where(r16 == c16, padded, 0.0), axis=0, keepdims=True)
    segoff = jnp.sum(jnp.where(c16 < r16, prow, 0.0), axis=1, keepdims=True)
    cnt_ref[0] = jnp.broadcast_to(cnt, (N_EXPERTS, LANES)).astype(I32)
    ur = lax.broadcasted_iota(I32, (cw, cw), 0)
    uc = lax.broadcasted_iota(I32, (cw, cw), 1)
    upper = jnp.where(ur < uc, 1.0, 0.0).astype(BF16)
    carry = segoff
    for j in range(nb_tok // cw):
        sl = slice(j * cw, (j + 1) * cw)
        mc = m[:, sl]
        rank = _dot(mc.astype(BF16), upper) + carry
        pa_ref[0, :, sl] = jnp.sum(jnp.where(a_hot[:, sl], rank, 0.0), axis=0, keepdims=True).astype(I32)
        pb_ref[0, :, sl] = jnp.sum(jnp.where(b_hot[:, sl], rank, 0.0), axis=0, keepdims=True).astype(I32)
        carry = carry + jnp.sum(mc, axis=1, keepdims=True)


def _moe_sort(e1, e2, tmx):
    nb, _, nbt = e1.shape
    cw = min(nbt, 256)
    tok = pl.BlockSpec((1, 1, nbt), lambda i: (i, 0, 0))
    return pl.pallas_call(
        functools.partial(_sort_body, tmx=tmx, cw=cw),
        grid=(nb,),
        in_specs=[tok, tok],
        out_specs=[tok, tok, pl.BlockSpec((1, N_EXPERTS, LANES), lambda i: (i, 0, 0))],
        out_shape=[jax.ShapeDtypeStruct((nb, 1, nbt), I32), jax.ShapeDtypeStruct((nb, 1, nbt), I32),
                   jax.ShapeDtypeStruct((nb, N_EXPERTS, LANES), I32)],
        compiler_params=_cparams(("parallel",)),
        name="moe_sort",
    )(e1, e2)


def _dispatch_body(pa_ref, pb_ref, src_ref, dst_ref):
    nbt = src_ref.shape[1]
    dst_ref[...] = jnp.zeros_like(dst_ref)

    def body(i, carry):
        row = src_ref[0, pl.ds(i, 1), :]
        dst_ref[0, pl.ds(pa_ref[0, 0, i], 1), :] = row
        dst_ref[0, pl.ds(pb_ref[0, 0, i], 1), :] = row
        return carry

    lax.fori_loop(0, nbt, body, 0, unroll=min(8, nbt))


def _moe_dispatch(pa, pb, h2p, rp):
    nb, nbt, w = h2p.shape
    smem = pl.BlockSpec((1, 1, nbt), lambda i: (i, 0, 0), memory_space=pltpu.SMEM)
    return pl.pallas_call(
        _dispatch_body,
        grid=(nb,),
        in_specs=[smem, smem, pl.BlockSpec((1, nbt, w), lambda i: (i, 0, 0))],
        out_specs=pl.BlockSpec((1, rp, w), lambda i: (i, 0, 0)),
        out_shape=jax.ShapeDtypeStruct((nb, rp, w), U32),
        compiler_params=_cparams(("parallel",), 56),
        name="moe_dispatch",
    )(pa, pb, h2p)


def _expert_body(tb_ref, tr_ref, te_ref, tv_ref, tf_ref, x_ref, w1_ref, w3_ref, w2_ref, o_ref, w1b, w3b, w2b):
    t = pl.program_id(0)

    @pl.when(tf_ref[t] == 1)
    def _():
        w1b[...] = w1_ref[0].astype(BF16)
        w3b[...] = w3_ref[0].astype(BF16)
        w2b[...] = w2_ref[0].astype(BF16)

    @pl.when(tv_ref[t] == 1)
    def _():
        lo, hi = _unpack_pairs(x_ref[0])
        lo = lo.astype(BF16)
        hi = hi.astype(BF16)
        half = lo.shape[1]
        a = _dot(lo, w1b[0:half, :]) + _dot(hi, w1b[half:2 * half, :])
        g = _dot(lo, w3b[0:half, :]) + _dot(hi, w3b[half:2 * half, :])
        hact = (_silu(a) * g).astype(BF16)
        o_ref[0] = _pack_pairs(_dot(hact, w2b[...]))


def _moe_experts(tables, xs, w1, w3, w2, tmx):
    nb, rp, half = xs.shape
    ne, d, ff = w1.shape
    nt = tables[0].shape[0]
    xspec = pl.BlockSpec((1, tmx, half), lambda t, tb, tr, te, tv, tf: (tb[t], tr[t], 0))
    grid_spec = pltpu.PrefetchScalarGridSpec(
        num_scalar_prefetch=5,
        grid=(nt,),
        in_specs=[xspec,
                  pl.BlockSpec((1, d, ff), lambda t, tb, tr, te, tv, tf: (te[t], 0, 0)),
                  pl.BlockSpec((1, d, ff), lambda t, tb, tr, te, tv, tf: (te[t], 0, 0)),
                  pl.BlockSpec((1, ff, d), lambda t, tb, tr, te, tv, tf: (te[t], 0, 0))],
        out_specs=xspec,
        scratch_shapes=[pltpu.VMEM((d, ff), BF16), pltpu.VMEM((d, ff), BF16), pltpu.VMEM((ff, d), BF16)],
    )
    return pl.pallas_call(
        _expert_body,
        grid_spec=grid_spec,
        out_shape=jax.ShapeDtypeStruct((nb, rp, half), U32),
        compiler_params=_cparams(("arbitrary",)),
        name="moe_experts",
    )(*tables, xs, w1, w3, w2)


def _combine_body(pa_ref, pb_ref, wa_ref, wb_ref, ys_ref, x1_ref, g2_ref, lng_ref, lnb_ref, o_ref, ybuf):
    tc = x1_ref.shape[1]
    half = ys_ref.shape[2]

    def body(i, carry):
        alo, ahi = _unpack_pairs(ys_ref[0, pl.ds(pa_ref[0, 0, i], 1), :])
        blo, bhi = _unpack_pairs(ys_ref[0, pl.ds(pb_ref[0, 0, i], 1), :])
        wa = wa_ref[0, 0, i]
        wb = wb_ref[0, 0, i]
        ybuf[pl.ds(i, 1), 0:half] = wa * alo + wb * blo
        ybuf[pl.ds(i, 1), half:2 * half] = wa * ahi + wb * bhi
        return carry

    lax.fori_loop(0, tc, body, 0, unroll=min(8, tc))
    y = DN_ALPHA * x1_ref[0] + (1.0 + g2_ref[0]) * ybuf[...]
    o_ref[0] = _layer_norm(y, lng_ref[...], lnb_ref[...])


def _moe_combine(pa, pb, wa, wb, ys, x1, mod, lng, lnb, tc):
    nb, nbt, d = x1.shape
    rp, half = ys.shape[1:]
    smem = pl.BlockSpec((1, 1, tc), lambda i, j: (i, 0, j), memory_space=pltpu.SMEM)
    return pl.pallas_call(
        _combine_body,
        grid=(nb, nbt // tc),
        in_specs=[smem, smem, smem, smem,
                  pl.BlockSpec((1, rp, half), lambda i, j: (i, 0, 0)),
                  pl.BlockSpec((1, tc, d), lambda i, j: (i, j, 0)),
                  _mod_spec(mod, tc, 5),
                  pl.BlockSpec((1, d), lambda i, j: (0, 0)),
                  pl.BlockSpec((1, d), lambda i, j: (0, 0))],
        out_specs=pl.BlockSpec((1, tc, d), lambda i, j: (i, j, 0)),
        out_shape=jax.ShapeDtypeStruct((nb, nbt, d), F32),
        scratch_shapes=[pltpu.VMEM((tc, d), F32)],
        compiler_params=_cparams(("parallel", "arbitrary"), 56),
        name="moe_combine",
    )(pa, pb, wa, wb, ys, x1, mod, lng, lnb)


def _tile_tables(cnt, tmx, max_tiles):
    nb = cnt.shape[0]
    nt = (cnt + (tmx - 1)) // tmx
    first_tile = jnp.cumsum(nt, axis=1) - nt
    nt_eb = nt.T.reshape(-1)
    ends = jnp.cumsum(nt_eb)
    total = ends[-1]
    t = jnp.arange(max_tiles, dtype=I32)
    valid = t < total
    tc = jnp.minimum(t, total - 1)
    seg = jnp.searchsorted(ends, tc, side="right").astype(I32)
    within = tc - (ends[seg] - nt_eb[seg])
    te = seg // nb
    tb = seg % nb
    tr = first_tile[tb, te] + within
    prev = jnp.concatenate([jnp.full((1,), -1, I32), te[:-1]])
    tf = (valid & (te != prev)).astype(I32)
    return tb.astype(I32), tr.astype(I32), te.astype(I32), valid.astype(I32), tf


def _moe(h2p, ri, x1, mod, w1, w3, w2, lng, lnb, nbt, tmx, tc):
    b, s, d = x1.shape
    nb = (b * s) // nbt
    rp = 2 * nbt + N_EXPERTS * tmx
    max_tiles = nb * (2 * nbt // tmx + N_EXPERTS)
    r = ri.reshape(nb, nbt, LANES)
    e1 = r[:, :, 0].astype(I32).reshape(nb, 1, nbt)
    e2 = r[:, :, 1].astype(I32).reshape(nb, 1, nbt)
    wa = r[:, :, 2].reshape(nb, 1, nbt)
    wb = r[:, :, 3].reshape(nb, 1, nbt)
    pa, pb, cnt = _moe_sort(e1, e2, tmx)
    tables = _tile_tables(cnt[:, :, 0], tmx, max_tiles)
    xs = _moe_dispatch(pa, pb, h2p.reshape(nb, nbt, d // 2), rp)
    ys = _moe_experts(tables, xs, w1, w3, w2, tmx)
    x2 = _moe_combine(pa, pb, wa, wb, ys, x1.reshape(nb, nbt, d), mod, lng, lnb, tc)
    return x2.reshape(b, s, d)


def _lambda(lam_ref):
    lp = lam_ref[...]
    a = jnp.sum(lp[0:1, :] * lp[1:2, :], axis=1, keepdims=True)
    c = jnp.sum(lp[2:3, :] * lp[3:4, :], axis=1, keepdims=True)
    return jnp.exp(a) - jnp.exp(c)


def _attn_body(slopes_ref, q_ref, k_ref, v_ref, lam_ref, o_ref, kb, vb, *, lam_init):
    h = pl.program_id(1)
    j = pl.program_id(2)
    tq = q_ref.shape[1]
    s_len = k_ref.shape[1]
    hd2 = q_ref.shape[2]

    @pl.when(j == 0)
    def _():
        kb[...] = k_ref[0].astype(BF16)
        vb[...] = v_ref[0].astype(BF16)

    lam = _lambda(lam_ref) + lam_init
    slope = slopes_ref[h]
    q = q_ref[0]
    lane = lax.broadcasted_iota(I32, (tq, hd2), 1)
    qpos = j * tq + lax.broadcasted_iota(I32, (tq, s_len), 0)
    kpos = lax.broadcasted_iota(I32, (tq, s_len), 1)
    dist = qpos - kpos
    bias = jnp.where(dist >= 0, -slope * dist.astype(F32), -jnp.inf)
    kk = kb[...]
    ps = []
    for c in range(2):
        qc = jnp.where((lane >= c * (hd2 // 2)) & (lane < (c + 1) * (hd2 // 2)), q, jnp.zeros_like(q))
        sc = _dot_nt(qc, kk) + bias
        e = jnp.exp(sc - jnp.max(sc, axis=1, keepdims=True))
        ps.append(e / jnp.sum(e, axis=1, keepdims=True))
    w = ps[0] - lam * ps[1]
    o_ref[0] = _dot(w.astype(BF16), vb[...])


def _attn_prompt(q, k, v, lam_p, slopes, lam_init, tq):
    b, s, d = q.shape
    heads = slopes.shape[0]
    hd2 = d // heads
    grid_spec = pltpu.PrefetchScalarGridSpec(
        num_scalar_prefetch=1,
        grid=(b, heads, s // tq),
        in_specs=[pl.BlockSpec((1, tq, hd2), lambda i, h, j, sl: (i, j, h)),
                  pl.BlockSpec((1, s, hd2), lambda i, h, j, sl: (i, 0, h)),
                  pl.BlockSpec((1, s, hd2), lambda i, h, j, sl: (i, 0, h)),
                  pl.BlockSpec(lam_p.shape, lambda i, h, j, sl: (0, 0))],
        out_specs=pl.BlockSpec((1, tq, hd2), lambda i, h, j, sl: (i, j, h)),
        scratch_shapes=[pltpu.VMEM((s, hd2), BF16), pltpu.VMEM((s, hd2), BF16)],
    )
    return pl.pallas_call(
        functools.partial(_attn_body, lam_init=lam_init),
        grid_spec=grid_spec,
        out_shape=jax.ShapeDtypeStruct((b, s, d), F32),
        compiler_params=_cparams(("parallel", "parallel", "arbitrary"), 56),
        name="attn_prompt",
    )(slopes, q, k, v, lam_p)


def _decode_body(pt_ref, q_ref, kn_ref, vn_ref, slope_ref, lam_ref, *rest, lam_init, past, heads):
    npg = PAGES_PER_STEP
    k_refs = rest[:npg]
    v_refs = rest[npg:2 * npg]
    o_ref, m_s, l_s, acc = rest[2 * npg:]
    j = pl.program_id(1)
    nj = pl.num_programs(1)
    d = q_ref.shape[2]
    rows = 2 * heads
    hd = d // rows
    r_i = lax.broadcasted_iota(I32, (rows, d), 0)
    l_i = lax.broadcasted_iota(I32, (rows, d), 1)
    own = (l_i >= r_i * hd) & (l_i < (r_i + 1) * hd)
    q16 = jnp.where(own, q_ref[0].astype(F32), 0.0)

    @pl.when(j == 0)
    def _():
        m_s[...] = jnp.sum(q16 * kn_ref[0], axis=1, keepdims=True)
        l_s[...] = jnp.ones_like(l_s)
        acc[...] = jnp.broadcast_to(vn_ref[0], acc.shape)

    q16b = q16.astype(BF16)
    s_parts = [_dot_nt(q16b, k_refs[i][0].astype(BF16)) for i in range(npg)]
    s = jnp.concatenate(s_parts, axis=1)
    kpos = j * (npg * PAGE_SIZE) + lax.broadcasted_iota(I32, s.shape, 1)
    s = s - slope_ref[...] * (past - kpos).astype(F32)
    m_old = m_s[...]
    m_new = jnp.maximum(m_old, jnp.max(s, axis=1, keepdims=True))
    a = jnp.exp(m_old - m_new)
    p = jnp.exp(s - m_new)
    l_s[...] = a * l_s[...] + jnp.sum(p, axis=1, keepdims=True)
    pb = p.astype(BF16)
    pv = _dot(pb[:, 0:PAGE_SIZE], v_refs[0][0].astype(BF16))
    for i in range(1, npg):
        pv = pv + _dot(pb[:, i * PAGE_SIZE:(i + 1) * PAGE_SIZE], v_refs[i][0].astype(BF16))
    acc[...] = a * acc[...] + pv
    m_s[...] = m_new

    @pl.when(j == nj - 1)
    def _():
        lam = _lambda(lam_ref) + lam_init
        o16 = acc[...] / l_s[...]
        head_of_lane = l_i // (2 * hd)
        o1 = jnp.sum(jnp.where(r_i == 2 * head_of_lane, o16, 0.0), axis=0, keepdims=True)
        o2 = jnp.sum(jnp.where(r_i == 2 * head_of_lane + 1, o16, 0.0), axis=0, keepdims=True)
        o_ref[0] = o1 - lam * o2


def _attn_decode(q, k_new, v_new, cache_k, cache_v, page_table, lam_p, slopes, lam_init):
    db, _, d = q.shape
    n_pages = page_table.shape[1]
    heads = slopes.shape[0]
    npg = PAGES_PER_STEP
    assert n_pages % npg == 0
    rows = 2 * heads
    slope16 = jnp.repeat(slopes, 2).reshape(rows, 1)
    tok = pl.BlockSpec((1, 1, d), lambda i, j, pt: (i, 0, 0))

    def page_spec(idx):
        return pl.BlockSpec((1, PAGE_SIZE, d), lambda i, j, pt: (pt[i, j * npg + idx], 0, 0))

    grid_spec = pltpu.PrefetchScalarGridSpec(
        num_scalar_prefetch=1,
        grid=(db, n_pages // npg),
        in_specs=[tok, tok, tok,
                  pl.BlockSpec((rows, 1), lambda i, j, pt: (0, 0)),
                  pl.BlockSpec(lam_p.shape, lambda i, j, pt: (0, 0))]
                 + [page_spec(i) for i in range(npg)] + [page_spec(i) for i in range(npg)],
        out_specs=tok,
        scratch_shapes=[pltpu.VMEM((rows, 1), F32), pltpu.VMEM((rows, 1), F32), pltpu.VMEM((rows, d), F32)],
    )
    return pl.pallas_call(
        functools.partial(_decode_body, lam_init=lam_init, past=n_pages * PAGE_SIZE, heads=heads),
        grid_spec=grid_spec,
        out_shape=jax.ShapeDtypeStruct((db, 1, d), F32),
        compiler_params=_cparams(("parallel", "arbitrary")),
        name="attn_decode",
    )(page_table, q, k_new, v_new, slope16, lam_p, *([cache_k] * npg), *([cache_v] * npg))


def _headwise_coefs(w, scale=1.0):
    nblk = w.shape[0]
    rows = []
    for delta in range(-3, 4):
        cols = []
        for dd in range(A_QKV_BLOCK):
            cc = dd + delta
            cols.append(w[:, cc, dd] if 0 <= cc < A_QKV_BLOCK else jnp.zeros((nblk,), F32))
        rows.append(jnp.stack(cols, axis=1).reshape(-1))
    rows.append(jnp.zeros_like(rows[0]))
    return jnp.stack(rows) * scale


def _prep_mlstm(conv_w, conv_b, w_q, w_k, w_v, w_gate, b_gate, heads):
    inner = conv_w.shape[1]
    hd = inner // heads
    ng = b_gate.shape[0]
    wg = w_gate.reshape(heads, 3, hd, ng).transpose(1, 0, 2, 3).reshape(3 * inner, ng)
    wgp = jnp.pad(wg, ((0, 0), (0, LANES - ng)))
    return dict(conv_w=conv_w, conv_b=conv_b.reshape(1, inner),
                cq=_headwise_coefs(w_q), ck=_headwise_coefs(w_k, hd ** -0.5), cv=_headwise_coefs(w_v),
                wg=wgp.astype(BF16), wgt=wg.T.astype(BF16),
                bg=jnp.pad(b_gate, (0, LANES - ng)).reshape(1, LANES), bgt=b_gate.reshape(ng, 1))


def _prep_router(w_group, b_group, w_expert, b_expert):
    d = w_group.shape[0]
    we = w_expert.transpose(1, 0, 2).reshape(d, N_EXPERTS)
    wr = jnp.pad(jnp.concatenate([w_group, we], axis=1), ((0, 0), (0, LANES - MOE_GROUPS - N_EXPERTS)))
    br = jnp.pad(jnp.concatenate([b_group, b_expert.reshape(-1)]), (0, LANES - MOE_GROUPS - N_EXPERTS))
    wrh = wr.astype(BF16)
    wrl = (wr - wrh.astype(F32)).astype(BF16)
    return wrh, wrl, br.reshape(1, LANES)


def _trunk(x, mod, conv0, state, cache, wts, cfg):
    b, s, d = x.shape
    heads_a = wts["heads_a"]
    ts = cfg["ts"]
    row_shape = cfg["row_shape"]
    xr = x.reshape(row_shape + (d,))

    m0 = mod[0]
    xm, z = _linear(xr, wts["a_w_in"], ts, (F32, F32), mod=m0, cols=(1, 0), name="a_in")
    inner = xm.shape[-1]
    hd = inner // heads_a
    tail8 = jnp.pad(conv0, ((0, 0), (8 - (A_CONV - 1), 0), (0, 0)))
    pre = _mlstm_pre(xm.reshape(b, s, inner), tail8, wts["a_prep"], cfg["ts_pre"], cfg["chunk"] is not None)
    if cfg["chunk"] is not None:
        xc, q, k, v, g, gt, conv8 = pre
        gcol = g.reshape(b, s, 2, heads_a).transpose(0, 3, 1, 2)
        grow = gt.reshape(b, 2, heads_a, s).transpose(0, 2, 1, 3)
        hh, c_new, n_new, m_new = _mlstm_scan(q, k, v, gcol, grow, cfg["chunk"])
        n_new = n_new.reshape(b, heads_a, hd)
        m_new = m_new.reshape(b, heads_a)
    else:
        xc, q, k, v, g, conv8 = pre
        c0, n0, mm0 = state
        g4 = g.reshape(b, 2, heads_a).transpose(0, 2, 1).reshape(b, heads_a, 1, 2)
        hh, c_new, n_new, m_new = _mlstm_step(q, k, v, g4, c0, n0.reshape(b, heads_a, 1, hd),
                                              mm0.reshape(b, heads_a, 1, 1))
        n_new = n_new.reshape(b, heads_a, hd)
        m_new = m_new.reshape(b, heads_a)
    conv_new = conv8[:, 8 - (A_CONV - 1):, :]
    rs = lambda a: a.reshape(row_shape + (a.shape[-1],))
    x1, h2p, ri = _mixer_out(
        functools.partial(_mixa_body, heads=heads_a), [rs(hh), rs(xc), z], xr, m0,
        [wts["a_norm_w"], wts["a_skip"], wts["a_w_out"], wts["ln_g"][0][0], wts["ln_b"][0][0], *wts["router"][0]],
        cfg["ts_out"], "mix_a")
    x2 = _moe(h2p, ri, x1, m0, *wts["moe"][0], wts["ln_g"][0][1], wts["ln_b"][0][1],
              cfg["nbt"], cfg["tmx"], cfg["tc"])

    k_sh, v_sh = _linear(x2, wts["kv_w"], ts, (F32, F32), name="kv_proj")

    m1 = mod[1]
    lam_init = 0.8 - 0.6 * math.exp(-0.3 * 1)
    (qq,) = _linear(x2, wts["b_w_q"], ts, (BF16,), mod=m1, cols=(1, 0), out_scale=wts["q_scale"], name="q_proj")
    if cache is None:
        o = _attn_prompt(qq.reshape(b, s, d), k_sh.reshape(b, s, d), v_sh.reshape(b, s, d),
                         wts["b_lambda"], wts["slopes"], lam_init, cfg["tq"])
    else:
        cache_k, cache_v, page_table = cache
        o = _attn_decode(qq.reshape(b, s, d), k_sh.reshape(b, s, d), v_sh.reshape(b, s, d),
                         cache_k, cache_v, page_table, wts["b_lambda"], wts["slopes"], lam_init)
    heads_b = wts["slopes"].shape[0]
    x3, h2p, ri = _mixer_out(
        functools.partial(_mixb_body, heads=heads_b, gain=1.0 - lam_init), [rs(o)], x2, m1,
        [wts["b_subln_w"], wts["b_w_out"], wts["ln_g"][1][0], wts["ln_b"][1][0], *wts["router"][1]],
        cfg["ts_out"], "mix_b")
    y = _moe(h2p, ri, x3, m1, *wts["moe"][1], wts["ln_g"][1][1], wts["ln_b"][1][1],
             cfg["nbt"], cfg["tmx"], cfg["tc"])
    return (y.reshape(b, s, d), k_sh.reshape(b, s, d), v_sh.reshape(b, s, d), conv_new, c_new, n_new, m_new)


def kernel(x_prompt, x_sample, cache_k, cache_v, state_conv, state_C, state_n, state_m, page_table, c_prompt, c_sample, a_w_in, a_conv_w, a_conv_b, a_w_q, a_w_k, a_w_v, a_w_gate, a_b_gate, a_norm_w, a_skip, a_w_out, kv_w_k, kv_w_v, b_w_q, b_lambda, b_subln_w, b_w_out, moe_w_group, moe_b_group, moe_w_expert, moe_b_expert, moe_w1, moe_w3, moe_w2, ln_g, ln_b, ada_w, ada_b):
    bp, sp, d = x_prompt.shape
    db, ds, _ = x_sample.shape
    assert ds == 1
    heads_a = state_C.shape[2]
    heads_b = cache_k.shape[2]
    inner = a_conv_w.shape[2]
    hd_b = d // (2 * heads_b)
    ff = moe_w1.shape[-1]

    slopes = jnp.exp2(-8.0 * jnp.arange(1, heads_b + 1, dtype=F32) / heads_b)
    wts = dict(
        heads_a=heads_a,
        a_w_in=a_w_in[0].astype(BF16),
        a_prep=_prep_mlstm(a_conv_w[0], a_conv_b[0], a_w_q[0], a_w_k[0], a_w_v[0], a_w_gate[0], a_b_gate[0], heads_a),
        a_norm_w=a_norm_w[0].reshape(1, inner), a_skip=a_skip[0].reshape(1, inner),
        a_w_out=a_w_out[0].astype(BF16),
        kv_w=jnp.concatenate([kv_w_k, kv_w_v], axis=1).astype(BF16),
        b_w_q=b_w_q[0].astype(BF16), q_scale=hd_b ** -0.5,
        b_lambda=b_lambda[0], b_subln_w=b_subln_w[0].reshape(1, -1), b_w_out=b_w_out[0].astype(BF16),
        slopes=slopes,
        router=[_prep_router(moe_w_group[i], moe_b_group[i], moe_w_expert[i], moe_b_expert[i]) for i in range(DEPTH)],
        moe=[(moe_w1[i].reshape(N_EXPERTS, d, ff), moe_w3[i].reshape(N_EXPERTS, d, ff),
              moe_w2[i].reshape(N_EXPERTS, ff, d)) for i in range(DEPTH)],
        ln_g=[[ln_g[i, j].reshape(1, d) for j in range(2)] for i in range(DEPTH)],
        ln_b=[[ln_b[i, j].reshape(1, d) for j in range(2)] for i in range(DEPTH)],
    )

    mod = _ada(jnp.concatenate([c_prompt, c_sample], axis=0), ada_w, ada_b)
    mod_p = mod[:, :bp].reshape(DEPTH, bp, 1, 6 * d)
    mod_s = mod[:, bp:].reshape(DEPTH, 1, db, 6 * d)

    cfg_p = dict(ts=512, ts_pre=256, ts_out=256, chunk=256, row_shape=(bp, sp), nbt=sp, tmx=128, tc=512, tq=256)
    zero_conv = jnp.zeros((bp, A_CONV - 1, inner), F32)
    outs_p = _trunk(x_prompt, mod_p, zero_conv, None, None, wts, cfg_p)

    cfg_s = dict(ts=db, ts_pre=1, ts_out=db, chunk=None, row_shape=(1, db), nbt=db, tmx=16, tc=db, tq=None)
    n_pool = cache_k.shape[0]
    cache = (cache_k.reshape(n_pool, PAGE_SIZE, d), cache_v.reshape(n_pool, PAGE_SIZE, d), page_table)
    outs_s = _trunk(x_sample, mod_s, state_conv[0], (state_C[0], state_n[0], state_m[0]), cache, wts, cfg_s)

    def pack(o, nb_, s_):
        y, k, v, conv, c, n, m = o
        return (y, k.reshape(nb_, s_, heads_b, 2 * hd_b), v.reshape(nb_, s_, heads_b, 2 * hd_b),
                conv[None], c[None], n[None], m[None])

    yp, kp, vp, convp, cp, np_, mp = pack(outs_p, bp, sp)
    ys, ks, vs, convs, cs, ns, ms_ = pack(outs_s, db, ds)
    return (yp, ys, kp, vp, convp, cp, np_, mp, ks, vs, convs, cs, ns, ms_)
```

```python
import functools
import math

import jax
import jax.numpy as jnp
from jax import lax
from jax.experimental import pallas as pl
from jax.experimental.pallas import tpu as pltpu

F32 = jnp.float32
BF16 = jnp.bfloat16
I32 = jnp.int32
U32 = jnp.uint32

DEPTH = 2
LN_EPS = 1e-5
DN_ALPHA = (2.0 * DEPTH) ** 0.25
A_CONV = 4
A_QKV_BLOCK = 4
MOE_GROUPS = 4
MOE_EXPERTS = 4
N_EXPERTS = MOE_GROUPS * MOE_EXPERTS
PAGE_SIZE = 128
LANES = 128
PAGES_PER_STEP = 8
MIB = 1024 * 1024


def _cparams(sem, vmem_mib=48):
    return pltpu.CompilerParams(dimension_semantics=sem, vmem_limit_bytes=vmem_mib * MIB)


def _silu(x):
    return x / (1.0 + jnp.exp(-x))


def _log_sigmoid(x):
    return jnp.minimum(x, 0.0) - jnp.log(1.0 + jnp.exp(-jnp.abs(x)))


def _dot(a, b):
    return jnp.dot(a, b, preferred_element_type=F32)


def _dot_nt(a, b):
    return lax.dot_general(a, b, (((1,), (1,)), ((), ())), preferred_element_type=F32)


def _dot_tn(a, b):
    return lax.dot_general(a, b, (((0,), (0,)), ((), ())), preferred_element_type=F32)


def _split(x):
    hi = x.astype(BF16)
    return hi, (x - hi.astype(F32)).astype(BF16)


def _dot3(a, w, dot=None):
    dot = dot or _dot
    ah, al = _split(a)
    wh, wl = _split(w)
    return dot(ah, wh) + dot(al, wh) + dot(ah, wl)


def _mm(a, w):
    if w.dtype == F32:
        return _dot3(a.astype(F32), w)
    return _dot(a.astype(BF16), w)


def _bits(x):
    return lax.bitcast_convert_type(x, U32)


def _pack_pairs(y):
    w = y.shape[-1] // 2
    r = _bits(y.astype(BF16).astype(F32))
    return (r[:, w:] & jnp.uint32(0xFFFF0000)) | (r[:, :w] >> 16)


def _unpack_pairs(u):
    lo = lax.bitcast_convert_type(u << 16, F32)
    hi = lax.bitcast_convert_type(u & jnp.uint32(0xFFFF0000), F32)
    return lo, hi


def _layer_norm(y, g, b):
    mu = jnp.mean(y, axis=-1, keepdims=True)
    yc = y - mu
    var = jnp.mean(yc * yc, axis=-1, keepdims=True)
    return yc * lax.rsqrt(var + LN_EPS) * g + b


def _ada_body(c_ref, w_ref, b_ref, o_ref):
    s = _silu(c_ref[...])
    o_ref[0] = _dot3(s, w_ref[0]) + b_ref[0]


def _ada(c_all, ada_w, ada_b):
    r, d = c_all.shape
    depth, _, n = ada_w.shape
    tn = 1536
    return pl.pallas_call(
        _ada_body,
        grid=(depth, n // tn),
        in_specs=[pl.BlockSpec((r, d), lambda i, j: (0, 0)),
                  pl.BlockSpec((1, d, tn), lambda i, j: (i, 0, j)),
                  pl.BlockSpec((1, 1, tn), lambda i, j: (i, 0, j))],
        out_specs=pl.BlockSpec((1, r, tn), lambda i, j: (i, 0, j)),
        out_shape=jax.ShapeDtypeStruct((depth, r, n), F32),
        compiler_params=_cparams(("parallel", "parallel")),
        name="ada",
    )(c_all, ada_w, ada_b.reshape(depth, 1, n))


def _mod_spec(mod, ts, col):
    d = mod.shape[-1] // 6
    if mod.shape[1] == 1:
        return pl.BlockSpec((1, 1, d), lambda b, s, *_: (b, 0, col))
    return pl.BlockSpec((1, ts, d), lambda b, s, *_: (b, s, col))


def _lin_body(*refs, has_mod, out_scale):
    if has_mod:
        x_ref, sc_ref, sh_ref, w_ref, *o_refs = refs
        x = x_ref[0] * (1.0 + sc_ref[0]) + sh_ref[0]
    else:
        x_ref, w_ref, *o_refs = refs
        x = x_ref[0]
    y = _mm(x, w_ref[...])
    if out_scale != 1.0:
        y = y * out_scale
    n = y.shape[-1] // len(o_refs)
    for j, o_ref in enumerate(o_refs):
        o_ref[0] = y[:, j * n:(j + 1) * n].astype(o_ref.dtype)


def _linear(x, w, ts, out_dtypes, mod=None, cols=None, out_scale=1.0, tn=None, name="linear"):
    b, s, k = x.shape
    n = w.shape[1]
    no = n // len(out_dtypes)
    tn = tn or n
    assert n % tn == 0 and (tn == n or no % tn == 0)
    in_specs = [pl.BlockSpec((1, ts, k), lambda i, j, c: (i, j, 0))]
    args = [x]
    if mod is not None:
        in_specs += [_mod_spec(mod, ts, cols[0]), _mod_spec(mod, ts, cols[1])]
        args += [mod, mod]
    in_specs.append(pl.BlockSpec((k, tn), lambda i, j, c: (0, c)))
    args.append(w)
    if tn == n:
        out_specs = [pl.BlockSpec((1, ts, no), lambda i, j, c: (i, j, 0)) for _ in out_dtypes]
        out_shape = [jax.ShapeDtypeStruct((b, s, no), dt) for dt in out_dtypes]
    else:
        assert len(set(out_dtypes)) == 1
        out_specs = [pl.BlockSpec((1, ts, tn), lambda i, j, c: (i, j, c))]
        out_shape = [jax.ShapeDtypeStruct((b, s, n), out_dtypes[0])]
    outs = pl.pallas_call(
        functools.partial(_lin_body, has_mod=mod is not None, out_scale=out_scale),
        grid=(b, s // ts, n // tn),
        in_specs=in_specs,
        out_specs=out_specs,
        out_shape=out_shape,
        compiler_params=_cparams(("parallel", "parallel", "parallel"), 56),
        name=name,
    )(*args)
    if tn == n:
        return outs
    return [outs[0][:, :, g * no:(g + 1) * no] for g in range(len(out_dtypes))]


def _p1_body(xm_ref, tail_ref, cw_ref, cb_ref, wqk_ref, wv_ref, wg_ref, wgt_ref, bg_ref, bgt_ref,
             xc_ref, q_ref, k_ref, v_ref, g_ref, *rest, want_gt):
    if want_gt:
        gt_ref, conv_ref, xbuf = rest
    else:
        conv_ref, xbuf = rest
    s = pl.program_id(1)
    ts = xm_ref.shape[1]
    inner = xm_ref.shape[2]

    @pl.when(s == 0)
    def _():
        xbuf[0:8, :] = tail_ref[0]

    if ts >= 8:
        @pl.when(s > 0)
        def _():
            xbuf[0:8, :] = xbuf[ts:ts + 8, :]

    xm = xm_ref[0]
    xbuf[8:8 + ts, :] = xm
    conv = cb_ref[...] + xm * cw_ref[A_CONV - 1:A_CONV, :]
    for j in range(A_CONV - 1):
        conv = conv + xbuf[5 + j:5 + j + ts, :] * cw_ref[j:j + 1, :]
    conv_ref[0] = xbuf[ts:ts + 8, :]
    xc = _silu(conv)
    xc_ref[0] = xc
    for c in range(inner // LANES):
        sl = slice(c * LANES, (c + 1) * LANES)
        qk = _mm(xc[:, sl], wqk_ref[c])
        q_ref[0, :, sl] = qk[:, 0:LANES].astype(q_ref.dtype)
        k_ref[0, :, sl] = qk[:, LANES:2 * LANES].astype(k_ref.dtype)
        v_ref[0, :, sl] = _mm(xm[:, sl], wv_ref[c]).astype(v_ref.dtype)
    q = q_ref[0]
    k = k_ref[0]
    v = v_ref[0]
    g = (_mm(q, wg_ref[0:inner, :]) + _mm(k, wg_ref[inner:2 * inner, :])
         + _mm(v, wg_ref[2 * inner:3 * inner, :]) + bg_ref[...])
    half = wgt_ref.shape[0] // 2
    lane = lax.broadcasted_iota(I32, g.shape, 1)
    g_ref[0] = jnp.where(lane >= half, _log_sigmoid(g), g)[:, 0:8]
    if want_gt:
        gt = (_dot_nt(wgt_ref[:, 0:inner], q) + _dot_nt(wgt_ref[:, inner:2 * inner], k)
              + _dot_nt(wgt_ref[:, 2 * inner:3 * inner], v) + bgt_ref[...])
        sub = lax.broadcasted_iota(I32, gt.shape, 0)
        gt_ref[0] = jnp.where(sub >= half, _log_sigmoid(gt), gt)


def _mlstm_pre(xm, tail8, prep, ts, want_gt):
    b, s, inner = xm.shape
    assert s == ts or ts >= 8
    full = lambda shape: pl.BlockSpec(shape, lambda i, j: (0,) * len(shape))
    row = lambda n, dt: (pl.BlockSpec((1, ts, n), lambda i, j: (i, j, 0)), jax.ShapeDtypeStruct((b, s, n), dt))
    qdt = BF16 if prep["wqk"].dtype == BF16 else F32
    outs = [row(inner, F32), row(inner, qdt), row(inner, qdt), row(inner, qdt), row(8, F32)]
    if want_gt:
        outs.append((pl.BlockSpec((1, 8, ts), lambda i, j: (i, 0, j)), jax.ShapeDtypeStruct((b, 8, s), F32)))
    outs.append((pl.BlockSpec((1, 8, inner), lambda i, j: (i, 0, 0)), jax.ShapeDtypeStruct((b, 8, inner), F32)))
    return pl.pallas_call(
        functools.partial(_p1_body, want_gt=want_gt),
        grid=(b, s // ts),
        in_specs=[pl.BlockSpec((1, ts, inner), lambda i, j: (i, j, 0)),
                  pl.BlockSpec((1, 8, inner), lambda i, j: (i, 0, 0)),
                  full((A_CONV, inner)), full((1, inner)),
                  full((inner // LANES, LANES, 2 * LANES)), full((inner // LANES, LANES, LANES)),
                  full((3 * inner, LANES)), full((8, 3 * inner)), full((1, LANES)), full((8, 1))],
        out_specs=[o[0] for o in outs],
        out_shape=[o[1] for o in outs],
        scratch_shapes=[pltpu.VMEM((ts + 8, inner), F32)],
        compiler_params=_cparams(("parallel", "arbitrary"), 56),
        name="mlstm_pre",
    )(xm, tail8, prep["conv_w"], prep["conv_b"], prep["wqk"], prep["wv"],
      prep["wg"], prep["wgt"], prep["bg"], prep["bgt"])


def _p2_body(q_ref, k_ref, v_ref, gc_ref, gr_ref, h_ref, c_ref, n_ref, m_ref, caug, ms):
    c = pl.program_id(1)
    nc = pl.num_programs(1)
    L = q_ref.shape[1]
    heads = gc_ref.shape[1]
    hd = q_ref.shape[2] // heads

    @pl.when(c == 0)
    def _():
        caug[...] = jnp.zeros_like(caug)
        ms[...] = jnp.zeros_like(ms)

    row = lax.broadcasted_iota(I32, (L, L), 0)
    col = lax.broadcasted_iota(I32, (L, L), 1)
    causal = col <= row
    lane = lax.broadcasted_iota(I32, (L, LANES), 1)
    ones_col = jnp.where(lane == 0, 1.0, 0.0).astype(BF16)
    for a in range(heads):
        sl = slice(a * hd, (a + 1) * hd)
        q = q_ref[0, :, sl]
        k = k_ref[0, :, sl]
        v = v_ref[0, :, sl]
        gc = gc_ref[0, a]
        gr = gr_ref[0, a]
        ig_c, lf_c = gc[:, 0:1], gc[:, 1:2]
        ig_r, lf_r = gr[0:1, :], gr[1:2, :]
        b_c = jnp.sum(jnp.where(causal, lf_r, 0.0), axis=1, keepdims=True)
        b_r = jnp.sum(jnp.where(row <= col, lf_c, 0.0), axis=0, keepdims=True)
        log_d = jnp.where(causal, b_c - b_r + ig_r, -jnp.inf)
        m_prev = ms[a]
        log_inter = b_c + m_prev
        m_t = jnp.maximum(log_inter, jnp.max(log_d, axis=1, keepdims=True))
        d = jnp.exp(log_d - m_t)
        w_inter = jnp.exp(log_inter - m_t)
        s = (_dot_nt(q, k) * d).astype(BF16)
        vaug = jnp.concatenate([v, ones_col], axis=1)
        ca = caug[a]
        num = w_inter * _dot(q, ca.astype(BF16)) + _dot(s, vaug)
        den = num[:, hd:hd + 1]
        h_ref[0, :, sl] = num[:, 0:hd] / jnp.maximum(jnp.abs(den), jnp.exp(-m_t))
        m_new = m_t[L - 1:L, :]
        b_last = b_c[L - 1:L, :]
        w_s = jnp.exp(b_last - b_c + ig_c - m_new)
        decay = jnp.exp(b_last + m_prev - m_new)
        kw = (k.astype(F32) * w_s).astype(BF16)
        caug[a] = decay * ca + _dot_tn(kw, vaug)
        ms[a] = m_new

    @pl.when(c == nc - 1)
    def _():
        for a in range(heads):
            c_ref[0, a] = caug[a, :, 0:hd]
            n_ref[0, a] = caug[a, :, hd:hd + 1]
            m_ref[0, a] = ms[a]


def _mlstm_scan(q, k, v, gcol, grow, L):
    b, s, inner = q.shape
    heads = gcol.shape[1]
    hd = inner // heads
    qkv = pl.BlockSpec((1, L, inner), lambda i, c: (i, c, 0))
    return pl.pallas_call(
        _p2_body,
        grid=(b, s // L),
        in_specs=[qkv, qkv, qkv,
                  pl.BlockSpec((1, heads, L, 2), lambda i, c: (i, 0, c, 0)),
                  pl.BlockSpec((1, heads, 2, L), lambda i, c: (i, 0, 0, c))],
        out_specs=[qkv,
                   pl.BlockSpec((1, heads, hd, hd), lambda i, c: (i, 0, 0, 0)),
                   pl.BlockSpec((1, heads, hd, 1), lambda i, c: (i, 0, 0, 0)),
                   pl.BlockSpec((1, heads, 1, 1), lambda i, c: (i, 0, 0, 0))],
        out_shape=[jax.ShapeDtypeStruct((b, s, inner), F32),
                   jax.ShapeDtypeStruct((b, heads, hd, hd), F32),
                   jax.ShapeDtypeStruct((b, heads, hd, 1), F32),
                   jax.ShapeDtypeStruct((b, heads, 1, 1), F32)],
        scratch_shapes=[pltpu.VMEM((heads, hd, hd + LANES), F32), pltpu.VMEM((heads, 1, 1), F32)],
        compiler_params=_cparams(("parallel", "arbitrary")),
        name="mlstm_scan",
    )(q, k, v, gcol, grow)


def _s2_body(q_ref, k_ref, v_ref, g_ref, c_ref, n_ref, m_ref, h_ref, co_ref, no_ref, mo_ref):
    heads = c_ref.shape[1]
    hd = c_ref.shape[2]
    sub = lax.broadcasted_iota(I32, (8, hd), 0)
    first = sub == 0
    for a in range(heads):
        sl = slice(a * hd, (a + 1) * hd)
        q = q_ref[0, :, sl]
        k = k_ref[0, :, sl]
        v = v_ref[0, :, sl]
        g = g_ref[0, a]
        ig, lf = g[:, 0:1], g[:, 1:2]
        m_prev = m_ref[0, a]
        m_new = jnp.maximum(lf + m_prev, ig)
        decay = jnp.exp(lf + m_prev - m_new)
        dd = jnp.exp(ig - m_new)
        k8 = jnp.where(first, k, 0.0)
        v8 = jnp.where(first, v, 0.0)
        q8 = jnp.where(first, q, 0.0)
        c_new = decay * c_ref[0, a] + dd * _dot3(k8, v8, _dot_tn)
        n_new = decay * n_ref[0, a] + dd * k
        num = _dot3(q8, c_new)[0:1, :]
        den = jnp.sum(q * n_new, axis=1, keepdims=True)
        h_ref[0, :, sl] = num / jnp.maximum(jnp.abs(den), jnp.exp(-m_new))
        co_ref[0, a] = c_new
        no_ref[0, a] = n_new
        mo_ref[0, a] = m_new


def _mlstm_step(q, k, v, g4, c0, n0, m0):
    b, _, inner = q.shape
    heads = c0.shape[1]
    hd = inner // heads
    qkv = pl.BlockSpec((1, 1, inner), lambda i: (i, 0, 0))
    st = lambda r, c: pl.BlockSpec((1, heads, r, c), lambda i: (i, 0, 0, 0))
    return pl.pallas_call(
        _s2_body,
        grid=(b,),
        in_specs=[qkv, qkv, qkv, st(1, 2), st(hd, hd), st(1, hd), st(1, 1)],
        out_specs=[qkv, st(hd, hd), st(1, hd), st(1, 1)],
        out_shape=[jax.ShapeDtypeStruct((b, 1, inner), F32),
                   jax.ShapeDtypeStruct((b, heads, hd, hd), F32),
                   jax.ShapeDtypeStruct((b, heads, 1, hd), F32),
                   jax.ShapeDtypeStruct((b, heads, 1, 1), F32)],
        compiler_params=_cparams(("parallel",)),
        name="mlstm_step",
    )(q, k, v, g4, c0, n0, m0)


def _route(logits):
    ts = logits.shape[0]
    lane = lax.broadcasted_iota(I32, (ts, LANES), 1)
    neg = -jnp.inf
    big = jnp.int32(1 << 20)
    is_g = lane < MOE_GROUPS
    gl = jnp.where(is_g, logits, neg)
    gmax = jnp.max(gl, axis=1, keepdims=True)
    gidx = jnp.min(jnp.where(gl == gmax, lane, big), axis=1, keepdims=True)
    gsum = jnp.sum(jnp.where(is_g, jnp.exp(logits - gmax), 0.0), axis=1, keepdims=True)
    g_w = 1.0 / gsum
    lo = MOE_GROUPS + MOE_EXPERTS * gidx
    el = jnp.where(lane >= lo, jnp.where(lane < lo + MOE_EXPERTS, logits, neg), neg)
    v1 = jnp.max(el, axis=1, keepdims=True)
    i1 = jnp.min(jnp.where(el == v1, lane, big), axis=1, keepdims=True)
    el2 = jnp.where(lane == i1, neg, el)
    v2 = jnp.max(el2, axis=1, keepdims=True)
    i2 = jnp.min(jnp.where(el2 == v2, lane, big), axis=1, keepdims=True)
    t = jnp.exp(v2 - v1)
    w1 = g_w / (1.0 + t)
    w2 = g_w * t / (1.0 + t)
    e1 = (i1 - MOE_GROUPS).astype(F32)
    e2 = (i2 - MOE_GROUPS).astype(F32)
    return jnp.where(lane == 0, e1, jnp.where(lane == 1, e2, jnp.where(lane == 2, w1, jnp.where(lane == 3, w2, 0.0))))


def _epilogue(x, o, g1, sc2, sh2, lng, lnb, wrh, wrl, br, x1_ref, h2_ref, ri_ref):
    x1 = _layer_norm(DN_ALPHA * x + (1.0 + g1) * o, lng, lnb)
    x1_ref[0] = x1
    h2 = x1 * (1.0 + sc2) + sh2
    h2_ref[0] = _pack_pairs(h2) if h2_ref.dtype == U32 else h2
    xh = h2.astype(BF16)
    xl = (h2 - xh.astype(F32)).astype(BF16)
    logits = _dot(xh, wrh) + _dot(xl, wrh) + _dot(xh, wrl) + br
    ri_ref[0] = _route(logits)


def _mixa_body(h_ref, xc_ref, z_ref, x_ref, g1_ref, sc2_ref, sh2_ref, nw_ref, skip_ref, wo_ref,
               lng_ref, lnb_ref, wrh_ref, wrl_ref, br_ref, x1_ref, h2_ref, ri_ref, *, heads):
    h = h_ref[0]
    inner = h.shape[1]
    hd = inner // heads
    parts = []
    for a in range(heads):
        ha = h[:, a * hd:(a + 1) * hd]
        hc = ha - jnp.mean(ha, axis=1, keepdims=True)
        parts.append(hc * lax.rsqrt(jnp.mean(hc * hc, axis=1, keepdims=True) + LN_EPS))
    hn = jnp.concatenate(parts, axis=1) * nw_ref[...]
    z = z_ref[0]
    out = (hn + skip_ref[...] * xc_ref[0]) * (1.0 / (1.0 + jnp.exp(-z)))
    o = _mm(out, wo_ref[...])
    _epilogue(x_ref[0], o, g1_ref[0], sc2_ref[0], sh2_ref[0], lng_ref[...], lnb_ref[...],
              wrh_ref[...], wrl_ref[...], br_ref[...], x1_ref, h2_ref, ri_ref)


def _mixb_body(o_ref, x_ref, g1_ref, sc2_ref, sh2_ref, sw_ref, wo_ref,
               lng_ref, lnb_ref, wrh_ref, wrl_ref, br_ref, x1_ref, h2_ref, ri_ref, *, heads, gain):
    o = o_ref[0]
    vd = o.shape[1] // heads
    parts = []
    for a in range(heads):
        oa = o[:, a * vd:(a + 1) * vd]
        parts.append(oa * lax.rsqrt(jnp.mean(oa * oa, axis=1, keepdims=True) + LN_EPS) * sw_ref[...] * gain)
    on = jnp.concatenate(parts, axis=1)
    y = _mm(on, wo_ref[...])
    _epilogue(x_ref[0], y, g1_ref[0], sc2_ref[0], sh2_ref[0], lng_ref[...], lnb_ref[...],
              wrh_ref[...], wrl_ref[...], br_ref[...], x1_ref, h2_ref, ri_ref)


def _mixer_out(body, acts, x, mod, consts, ts, packed, name):
    b, s, d = x.shape
    hw, hdt = (d // 2, U32) if packed else (d, F32)
    rowspec = lambda a: pl.BlockSpec((1, ts, a.shape[2]), lambda i, j: (i, j, 0))
    full = lambda a: pl.BlockSpec(a.shape, lambda i, j: (0,) * a.ndim)
    in_specs = [rowspec(a) for a in acts] + [rowspec(x)]
    in_specs += [_mod_spec(mod, ts, 2), _mod_spec(mod, ts, 4), _mod_spec(mod, ts, 3)]
    in_specs += [full(c) for c in consts]
    return pl.pallas_call(
        body,
        grid=(b, s // ts),
        in_specs=in_specs,
        out_specs=[pl.BlockSpec((1, ts, d), lambda i, j: (i, j, 0)),
                   pl.BlockSpec((1, ts, hw), lambda i, j: (i, j, 0)),
                   pl.BlockSpec((1, ts, LANES), lambda i, j: (i, j, 0))],
        out_shape=[jax.ShapeDtypeStruct((b, s, d), F32),
                   jax.ShapeDtypeStruct((b, s, hw), hdt),
                   jax.ShapeDtypeStruct((b, s, LANES), F32)],
        compiler_params=_cparams(("parallel", "parallel"), 56),
        name=name,
    )(*acts, x, mod, mod, mod, *consts)


def _sort_body(e1_ref, e2_ref, pa_ref, pb_ref, cnt_ref, *, tmx, cw):
    nb_tok = e1_ref.shape[2]
    e1 = e1_ref[0]
    e2 = e2_ref[0]
    sub = lax.broadcasted_iota(I32, (N_EXPERTS, nb_tok), 0)
    a_hot = sub == e1
    b_hot = sub == e2
    m = jnp.where(a_hot, 1.0, 0.0) + jnp.where(b_hot, 1.0, 0.0)
    cnt = jnp.sum(m, axis=1, keepdims=True)
    padded = jnp.floor((cnt + (tmx - 1)) / tmx) * tmx
    r16 = lax.broadcasted_iota(I32, (N_EXPERTS, N_EXPERTS), 0)
    c16 = lax.broadcasted_iota(I32, (N_EXPERTS, N_EXPERTS), 1)
    prow = jnp.sum(jnp.where(r16 == c16, padded, 0.0), axis=0, keepdims=True)
    segoff = jnp.sum(jnp.where(c16 < r16, prow, 0.0), axis=1, keepdims=True)
    cnt_ref[0] = jnp.broadcast_to(cnt, (N_EXPERTS, LANES)).astype(I32)
    ur = lax.broadcasted_iota(I32, (cw, cw), 0)
    uc = lax.broadcasted_iota(I32, (cw, cw), 1)
    upper = jnp.where(ur < uc, 1.0, 0.0).astype(BF16)
    carry = segoff
    for j in range(nb_tok // cw):
        sl = slice(j * cw, (j + 1) * cw)
        mc = m[:, sl]
        rank = _dot(mc.astype(BF16), upper) + carry
        pa_ref[0, :, sl] = jnp.sum(jnp.where(a_hot[:, sl], rank, 0.0), axis=0, keepdims=True).astype(I32)
        pb_ref[0, :, sl] = jnp.sum(jnp.where(b_hot[:, sl], rank, 0.0), axis=0, keepdims=True).astype(I32)
        carry = carry + jnp.sum(mc, axis=1, keepdims=True)


def _moe_sort(e1, e2, tmx):
    nb, _, nbt = e1.shape
    cw = min(nbt, 256)
    tok = pl.BlockSpec((1, 1, nbt), lambda i: (i, 0, 0))
    return pl.pallas_call(
        functools.partial(_sort_body, tmx=tmx, cw=cw),
        grid=(nb,),
        in_specs=[tok, tok],
        out_specs=[tok, tok, pl.BlockSpec((1, N_EXPERTS, LANES), lambda i: (i, 0, 0))],
        out_shape=[jax.ShapeDtypeStruct((nb, 1, nbt), I32), jax.ShapeDtypeStruct((nb, 1, nbt), I32),
                   jax.ShapeDtypeStruct((nb, N_EXPERTS, LANES), I32)],
        compiler_params=_cparams(("parallel",)),
        name="moe_sort",
    )(e1, e2)


def _dispatch_body(pa_ref, pb_ref, src_ref, dst_ref):
    nbt = src_ref.shape[1]
    dst_ref[...] = jnp.zeros_like(dst_ref)

    def body(i, carry):
        row = src_ref[0, pl.ds(i, 1), :]
        dst_ref[0, pl.ds(pa_ref[0, 0, i], 1), :] = row
        dst_ref[0, pl.ds(pb_ref[0, 0, i], 1), :] = row
        return carry

    lax.fori_loop(0, nbt, body, 0, unroll=min(8, nbt))


def _moe_dispatch(pa, pb, h2p, rp):
    nb, nbt, w = h2p.shape
    smem = pl.BlockSpec((1, 1, nbt), lambda i: (i, 0, 0), memory_space=pltpu.SMEM)
    return pl.pallas_call(
        _dispatch_body,
        grid=(nb,),
        in_specs=[smem, smem, pl.BlockSpec((1, nbt, w), lambda i: (i, 0, 0))],
        out_specs=pl.BlockSpec((1, rp, w), lambda i: (i, 0, 0)),
        out_shape=jax.ShapeDtypeStruct((nb, rp, w), U32),
        compiler_params=_cparams(("parallel",), 56),
        name="moe_dispatch",
    )(pa, pb, h2p)


def _expert_body(tb_ref, tr_ref, te_ref, tv_ref, tf_ref, x_ref, w1_ref, w3_ref, w2_ref, o_ref, w1b, w3b, w2b):
    t = pl.program_id(0)

    @pl.when(tf_ref[t] == 1)
    def _():
        w1b[...] = w1_ref[0].astype(BF16)
        w3b[...] = w3_ref[0].astype(BF16)
        w2b[...] = w2_ref[0].astype(BF16)

    @pl.when(tv_ref[t] == 1)
    def _():
        lo, hi = _unpack_pairs(x_ref[0])
        lo = lo.astype(BF16)
        hi = hi.astype(BF16)
        half = lo.shape[1]
        a = _dot(lo, w1b[0:half, :]) + _dot(hi, w1b[half:2 * half, :])
        g = _dot(lo, w3b[0:half, :]) + _dot(hi, w3b[half:2 * half, :])
        hact = (_silu(a) * g).astype(BF16)
        o_ref[0] = _pack_pairs(_dot(hact, w2b[...]))


def _moe_experts(tables, xs, w1, w3, w2, tmx):
    nb, rp, half = xs.shape
    ne, d, ff = w1.shape
    nt = tables[0].shape[0]
    xspec = pl.BlockSpec((1, tmx, half), lambda t, tb, tr, te, tv, tf: (tb[t], tr[t], 0))
    grid_spec = pltpu.PrefetchScalarGridSpec(
        num_scalar_prefetch=5,
        grid=(nt,),
        in_specs=[xspec,
                  pl.BlockSpec((1, d, ff), lambda t, tb, tr, te, tv, tf: (te[t], 0, 0)),
                  pl.BlockSpec((1, d, ff), lambda t, tb, tr, te, tv, tf: (te[t], 0, 0)),
                  pl.BlockSpec((1, ff, d), lambda t, tb, tr, te, tv, tf: (te[t], 0, 0))],
        out_specs=xspec,
        scratch_shapes=[pltpu.VMEM((d, ff), BF16), pltpu.VMEM((d, ff), BF16), pltpu.VMEM((ff, d), BF16)],
    )
    return pl.pallas_call(
        _expert_body,
        grid_spec=grid_spec,
        out_shape=jax.ShapeDtypeStruct((nb, rp, half), U32),
        compiler_params=_cparams(("arbitrary",)),
        name="moe_experts",
    )(*tables, xs, w1, w3, w2)


def _combine_body(pa_ref, pb_ref, ri_ref, ys_ref, x1_ref, g2_ref, lng_ref, lnb_ref, o_ref, ga, gb):
    tc = x1_ref.shape[1]

    def body(i, carry):
        ga[pl.ds(i, 1), :] = ys_ref[0, pl.ds(pa_ref[0, 0, i], 1), :]
        gb[pl.ds(i, 1), :] = ys_ref[0, pl.ds(pb_ref[0, 0, i], 1), :]
        return carry

    lax.fori_loop(0, tc, body, 0, unroll=min(8, tc))
    alo, ahi = _unpack_pairs(ga[...])
    blo, bhi = _unpack_pairs(gb[...])
    ri = ri_ref[0]
    wa = ri[:, 2:3]
    wb = ri[:, 3:4]
    ffn = jnp.concatenate([wa * alo + wb * blo, wa * ahi + wb * bhi], axis=1)
    y = DN_ALPHA * x1_ref[0] + (1.0 + g2_ref[0]) * ffn
    o_ref[0] = _layer_norm(y, lng_ref[...], lnb_ref[...])


def _moe_combine(pa, pb, ri, ys, x1, mod, lng, lnb, tc):
    nb, nbt, d = x1.shape
    rp, half = ys.shape[1:]
    smem = pl.BlockSpec((1, 1, tc), lambda i, j: (i, 0, j), memory_space=pltpu.SMEM)
    return pl.pallas_call(
        _combine_body,
        grid=(nb, nbt // tc),
        in_specs=[smem, smem,
                  pl.BlockSpec((1, tc, LANES), lambda i, j: (i, j, 0)),
                  pl.BlockSpec((1, rp, half), lambda i, j: (i, 0, 0)),
                  pl.BlockSpec((1, tc, d), lambda i, j: (i, j, 0)),
                  _mod_spec(mod, tc, 5),
                  pl.BlockSpec((1, d), lambda i, j: (0, 0)),
                  pl.BlockSpec((1, d), lambda i, j: (0, 0))],
        out_specs=pl.BlockSpec((1, tc, d), lambda i, j: (i, j, 0)),
        out_shape=jax.ShapeDtypeStruct((nb, nbt, d), F32),
        scratch_shapes=[pltpu.VMEM((tc, half), U32), pltpu.VMEM((tc, half), U32)],
        compiler_params=_cparams(("parallel", "arbitrary"), 56),
        name="moe_combine",
    )(pa, pb, ri, ys, x1, mod, lng, lnb)


def _tile_tables(cnt, tmx, max_tiles):
    nb = cnt.shape[0]
    nt = (cnt + (tmx - 1)) // tmx
    first_tile = jnp.cumsum(nt, axis=1) - nt
    nt_eb = nt.T.reshape(-1)
    ends = jnp.cumsum(nt_eb)
    total = ends[-1]
    t = jnp.arange(max_tiles, dtype=I32)
    valid = t < total
    tc = jnp.minimum(t, total - 1)
    seg = jnp.sum((ends[None, :] <= tc[:, None]).astype(I32), axis=1)
    within = tc - (ends[seg] - nt_eb[seg])
    te = seg // nb
    tb = seg % nb
    tr = first_tile[tb, te] + within
    prev = jnp.concatenate([jnp.full((1,), -1, I32), te[:-1]])
    tf = (valid & (te != prev)).astype(I32)
    return tb.astype(I32), tr.astype(I32), te.astype(I32), valid.astype(I32), tf


def _moe(h2p, ri, x1, mod, w1, w3, w2, lng, lnb, nbt, tmx, tc):
    b, s, d = x1.shape
    nb = (b * s) // nbt
    rp = 2 * nbt + N_EXPERTS * tmx
    max_tiles = nb * (2 * nbt // tmx + N_EXPERTS)
    r = ri.reshape(nb, nbt, LANES)
    e1 = r[:, :, 0].astype(I32).reshape(nb, 1, nbt)
    e2 = r[:, :, 1].astype(I32).reshape(nb, 1, nbt)
    pa, pb, cnt = _moe_sort(e1, e2, tmx)
    tables = _tile_tables(cnt[:, :, 0], tmx, max_tiles)
    xs = _moe_dispatch(pa, pb, h2p.reshape(nb, nbt, d // 2), rp)
    ys = _moe_experts(tables, xs, w1, w3, w2, tmx)
    x2 = _moe_combine(pa, pb, r, ys, x1.reshape(nb, nbt, d), mod, lng, lnb, tc)
    return x2.reshape(b, s, d)


def _moe_dense_body(h_ref, ri_ref, x1_ref, g2_ref, lng_ref, lnb_ref, w1_ref, w3_ref, w2_ref, o_ref, acc):
    e = pl.program_id(0)

    @pl.when(e == 0)
    def _():
        acc[...] = jnp.zeros_like(acc)

    x = h_ref[0]
    ri = ri_ref[0]
    ef = e.astype(F32)
    comb = jnp.where(ri[:, 0:1] == ef, ri[:, 2:3], 0.0) + jnp.where(ri[:, 1:2] == ef, ri[:, 3:4], 0.0)
    hact = _silu(_dot3(x, w1_ref[0])) * _dot3(x, w3_ref[0]) * comb
    acc[...] += _dot3(hact, w2_ref[0])

    @pl.when(e == pl.num_programs(0) - 1)
    def _():
        y = DN_ALPHA * x1_ref[0] + (1.0 + g2_ref[0]) * acc[...]
        o_ref[0] = _layer_norm(y, lng_ref[...], lnb_ref[...])


def _moe_dense(h2, ri, x1, mod, w1, w3, w2, lng, lnb):
    _, r, d = x1.shape
    ne, _, ff = w1.shape
    row = lambda n: pl.BlockSpec((1, r, n), lambda e: (0, 0, 0))
    return pl.pallas_call(
        _moe_dense_body,
        grid=(ne,),
        in_specs=[row(d), row(LANES), row(d),
                  pl.BlockSpec((1, r, d), lambda e: (0, 0, 5)),
                  pl.BlockSpec((1, d), lambda e: (0, 0)), pl.BlockSpec((1, d), lambda e: (0, 0)),
                  pl.BlockSpec((1, d, ff), lambda e: (e, 0, 0)), pl.BlockSpec((1, d, ff), lambda e: (e, 0, 0)),
                  pl.BlockSpec((1, ff, d), lambda e: (e, 0, 0))],
        out_specs=row(d),
        out_shape=jax.ShapeDtypeStruct((1, r, d), F32),
        scratch_shapes=[pltpu.VMEM((r, d), F32)],
        compiler_params=_cparams(("arbitrary",)),
        name="moe_dense",
    )(h2, ri, x1, mod, lng, lnb, w1, w3, w2)


def _lambda(lam_ref):
    lp = lam_ref[...]
    a = jnp.sum(lp[0:1, :] * lp[1:2, :], axis=1, keepdims=True)
    c = jnp.sum(lp[2:3, :] * lp[3:4, :], axis=1, keepdims=True)
    return jnp.exp(a) - jnp.exp(c)


POS_SPLIT = 16


def _attn_body(slopes_ref, q_ref, k_ref, v_ref, lam_ref, o_ref, kaug, vt, qaug, m_s, l_s, acc, *, lam_init, vd):
    hg = pl.program_id(1)
    j = pl.program_id(2)
    tq = q_ref.shape[1]
    group = q_ref.shape[2] // vd
    s_len = k_ref.shape[1]
    tk = tq

    @pl.when(j == 0)
    def _():
        pos = lax.broadcasted_iota(I32, (s_len, LANES), 0)
        ln = lax.broadcasted_iota(I32, (s_len, LANES), 1)
        hi = (pos // POS_SPLIT).astype(F32)
        lo = (pos % POS_SPLIT).astype(F32)
        ktail = jnp.where(ln == 0, hi, jnp.where(ln == 1, lo, 0.0)).astype(BF16)
        for g in range(group):
            kaug[g, :, 0:vd] = k_ref[0, :, g * vd:(g + 1) * vd].astype(BF16)
            kaug[g, :, vd:vd + LANES] = ktail
            vt[g] = v_ref[0, :, g * vd:(g + 1) * vd].T.astype(BF16)

    lane = lax.broadcasted_iota(I32, (tq, vd), 1)
    for g in range(group):
        slope = slopes_ref[hg * group + g]
        q = q_ref[0, :, g * vd:(g + 1) * vd].astype(F32)
        extra = jnp.where(lane == 0, slope * POS_SPLIT, jnp.where(lane == 1, slope, 0.0)).astype(BF16)
        qaug[g, 0:tq, 0:vd] = jnp.where(lane < vd // 2, q, 0.0).astype(BF16)
        qaug[g, 0:tq, vd:vd + LANES] = extra
        qaug[g, tq:2 * tq, 0:vd] = jnp.where(lane >= vd // 2, q, 0.0).astype(BF16)
        qaug[g, tq:2 * tq, vd:vd + LANES] = extra
    m_s[...] = jnp.full_like(m_s, -jnp.inf)
    l_s[...] = jnp.zeros_like(l_s)
    acc[...] = jnp.zeros_like(acc)

    def block(kb, masked):
        off = pl.multiple_of(kb * tk, tk)
        for g in range(group):
            st = _dot_nt(kaug[g, pl.ds(off, tk), :], qaug[g])
            if masked:
                r = lax.broadcasted_iota(I32, st.shape, 0)
                c = lax.broadcasted_iota(I32, st.shape, 1)
                st = jnp.where(r <= (c & (tq - 1)), st, -jnp.inf)
            m_old = m_s[g]
            m_new = jnp.maximum(m_old, jnp.max(st, axis=0, keepdims=True))
            a = jnp.exp(m_old - m_new)
            p = jnp.exp(st - m_new)
            l_s[g] = a * l_s[g] + jnp.sum(p, axis=0, keepdims=True)
            acc[g] = a * acc[g] + _dot(vt[g, :, pl.ds(off, tk)], p.astype(BF16))
            m_s[g] = m_new

    def off_diagonal(kb, carry):
        block(kb, False)
        return carry

    lax.fori_loop(0, j, off_diagonal, 0)
    block(j, True)
    lam = _lambda(lam_ref) + lam_init
    for g in range(group):
        ot = acc[g, :, 0:tq] / l_s[g, :, 0:tq] - lam * (acc[g, :, tq:2 * tq] / l_s[g, :, tq:2 * tq])
        o_ref[0, :, g * vd:(g + 1) * vd] = ot.T


def _attn_prompt(q, k, v, lam_p, slopes, lam_init, tq, group):
    b, s, d = q.shape
    heads = slopes.shape[0]
    vd = d // heads
    gw = group * vd
    assert heads % group == 0 and tq & (tq - 1) == 0
    assert s // POS_SPLIT <= 256
    grid_spec = pltpu.PrefetchScalarGridSpec(
        num_scalar_prefetch=1,
        grid=(b, heads // group, s // tq),
        in_specs=[pl.BlockSpec((1, tq, gw), lambda i, h, j, sl: (i, j, h)),
                  pl.BlockSpec((1, s, gw), lambda i, h, j, sl: (i, 0, h)),
                  pl.BlockSpec((1, s, gw), lambda i, h, j, sl: (i, 0, h)),
                  pl.BlockSpec(lam_p.shape, lambda i, h, j, sl: (0, 0))],
        out_specs=pl.BlockSpec((1, tq, gw), lambda i, h, j, sl: (i, j, h)),
        scratch_shapes=[pltpu.VMEM((group, s, vd + LANES), BF16), pltpu.VMEM((group, vd, s), BF16),
                        pltpu.VMEM((group, 2 * tq, vd + LANES), BF16),
                        pltpu.VMEM((group, 1, 2 * tq), F32), pltpu.VMEM((group, 1, 2 * tq), F32),
                        pltpu.VMEM((group, vd, 2 * tq), F32)],
    )
    return pl.pallas_call(
        functools.partial(_attn_body, lam_init=lam_init, vd=vd),
        grid_spec=grid_spec,
        out_shape=jax.ShapeDtypeStruct((b, s, d), F32),
        compiler_params=_cparams(("parallel", "parallel", "arbitrary"), 56),
        name="attn_prompt",
    )(slopes, q, k, v, lam_p)


def _decode_body(pt_ref, q_ref, kn_ref, vn_ref, slope_ref, lam_ref, *rest, lam_init, past, heads):
    npg = PAGES_PER_STEP
    k_refs = rest[:npg]
    v_refs = rest[npg:2 * npg]
    o_ref, m_s, l_s, acc = rest[2 * npg:]
    j = pl.program_id(1)
    nj = pl.num_programs(1)
    vd = q_ref.shape[3]
    rows = 2 * heads
    keys = PAGE_SIZE * heads
    lane = lax.broadcasted_iota(I32, (heads, vd), 1)
    qf = q_ref[0, 0].astype(F32)
    qm = jnp.concatenate([jnp.where(lane < vd // 2, qf, 0.0), jnp.where(lane >= vd // 2, qf, 0.0)], axis=0)

    @pl.when(j == 0)
    def _():
        kn = kn_ref[0, 0]
        vn = vn_ref[0, 0]
        m_s[...] = jnp.sum(qm * jnp.concatenate([kn, kn], axis=0), axis=1, keepdims=True)
        l_s[...] = jnp.ones_like(l_s)
        acc[...] = jnp.concatenate([vn, vn], axis=0)

    qh, ql = _split(qm)
    qhl = jnp.concatenate([qh, ql], axis=0)

    def scores(i):
        kh, kl = _split(k_refs[i][0].reshape(keys, vd))
        both = _dot_nt(qhl, kh)
        return both[0:rows] + both[rows:2 * rows] + _dot_nt(qh, kl)

    s = jnp.concatenate([scores(i) for i in range(npg)], axis=1)
    r_i = lax.broadcasted_iota(I32, s.shape, 0)
    c_i = lax.broadcasted_iota(I32, s.shape, 1)
    kpos = j * (npg * PAGE_SIZE) + c_i // heads
    bias = slope_ref[...] * (past - kpos).astype(F32)
    s = jnp.where((c_i % heads) == (r_i % heads), s - bias, -jnp.inf)
    m_old = m_s[...]
    m_new = jnp.maximum(m_old, jnp.max(s, axis=1, keepdims=True))
    a = jnp.exp(m_old - m_new)
    p = jnp.exp(s - m_new)
    l_s[...] = a * l_s[...] + jnp.sum(p, axis=1, keepdims=True)
    ph, plo = _split(p)
    phl = jnp.concatenate([ph, plo], axis=0)
    pv = jnp.zeros((rows, vd), F32)
    for i in range(npg):
        vh, vl = _split(v_refs[i][0].reshape(keys, vd))
        both = _dot(phl[:, i * keys:(i + 1) * keys], vh)
        pv = pv + both[0:rows] + both[rows:2 * rows] + _dot(ph[:, i * keys:(i + 1) * keys], vl)
    acc[...] = a * acc[...] + pv
    m_s[...] = m_new

    @pl.when(j == nj - 1)
    def _():
        lam = _lambda(lam_ref) + lam_init
        o16 = acc[...] / l_s[...]
        o_ref[0, 0] = o16[0:heads, :] - lam * o16[heads:rows, :]


def _attn_decode(q, k_new, v_new, cache_k, cache_v, page_table, lam_p, slopes, lam_init):
    db, _, heads, vd = q.shape
    n_pages = page_table.shape[1]
    npg = PAGES_PER_STEP
    assert n_pages % npg == 0
    rows = 2 * heads
    slope16 = jnp.concatenate([slopes, slopes]).reshape(rows, 1)
    tok = pl.BlockSpec((1, 1, heads, vd), lambda i, j, pt: (i, 0, 0, 0))

    def page_spec(idx):
        return pl.BlockSpec((1, PAGE_SIZE, heads, vd), lambda i, j, pt: (pt[i, j * npg + idx], 0, 0, 0))

    grid_spec = pltpu.PrefetchScalarGridSpec(
        num_scalar_prefetch=1,
        grid=(db, n_pages // npg),
        in_specs=[tok, tok, tok,
                  pl.BlockSpec((rows, 1), lambda i, j, pt: (0, 0)),
                  pl.BlockSpec(lam_p.shape, lambda i, j, pt: (0, 0))]
                 + [page_spec(i) for i in range(npg)] + [page_spec(i) for i in range(npg)],
        out_specs=tok,
        scratch_shapes=[pltpu.VMEM((rows, 1), F32), pltpu.VMEM((rows, 1), F32), pltpu.VMEM((rows, vd), F32)],
    )
    return pl.pallas_call(
        functools.partial(_decode_body, lam_init=lam_init, past=n_pages * PAGE_SIZE, heads=heads),
        grid_spec=grid_spec,
        out_shape=jax.ShapeDtypeStruct((db, 1, heads, vd), F32),
        compiler_params=_cparams(("parallel", "arbitrary")),
        name="attn_decode",
    )(page_table, q, k_new, v_new, slope16, lam_p, *([cache_k] * npg), *([cache_v] * npg))


def _headwise_blocks(w):
    per = LANES // A_QKV_BLOCK
    wc = w.reshape(-1, per, A_QKV_BLOCK, A_QKV_BLOCK)
    eye = jnp.eye(per, dtype=w.dtype)
    return jnp.einsum("cnij,nm->cnimj", wc, eye).reshape(-1, LANES, LANES)


def _prep_mlstm(conv_w, conv_b, w_q, w_k, w_v, w_gate, b_gate, heads, dt):
    inner = conv_w.shape[1]
    hd = inner // heads
    ng = b_gate.shape[0]
    wg = w_gate.reshape(heads, 3, hd, ng).transpose(1, 0, 2, 3).reshape(3 * inner, ng)
    wgp = jnp.pad(wg, ((0, 0), (0, LANES - ng)))
    return dict(conv_w=conv_w, conv_b=conv_b.reshape(1, inner),
                wqk=jnp.concatenate([_headwise_blocks(w_q), _headwise_blocks(w_k) * hd ** -0.5], axis=2).astype(dt),
                wv=_headwise_blocks(w_v).astype(dt),
                wg=wgp.astype(dt), wgt=wg.T.astype(BF16),
                bg=jnp.pad(b_gate, (0, LANES - ng)).reshape(1, LANES), bgt=b_gate.reshape(ng, 1))


def _prep_router(w_group, b_group, w_expert, b_expert):
    d = w_group.shape[0]
    we = w_expert.transpose(1, 0, 2).reshape(d, N_EXPERTS)
    wr = jnp.pad(jnp.concatenate([w_group, we], axis=1), ((0, 0), (0, LANES - MOE_GROUPS - N_EXPERTS)))
    br = jnp.pad(jnp.concatenate([b_group, b_expert.reshape(-1)]), (0, LANES - MOE_GROUPS - N_EXPERTS))
    wrh = wr.astype(BF16)
    wrl = (wr - wrh.astype(F32)).astype(BF16)
    return wrh, wrl, br.reshape(1, LANES)


def _trunk(x, mod, conv0, state, cache, wts, cfg):
    b, s, d = x.shape
    heads_a = wts["heads_a"]
    ts = cfg["ts"]
    row_shape = cfg["row_shape"]
    xr = x.reshape(row_shape + (d,))

    m0 = mod[0]
    tn = cfg["tn"]
    xm, z = _linear(xr, wts["a_w_in"], ts, (F32, F32), mod=m0, cols=(1, 0), tn=tn, name="a_in")
    inner = xm.shape[-1]
    hd = inner // heads_a
    tail8 = jnp.pad(conv0, ((0, 0), (8 - (A_CONV - 1), 0), (0, 0)))
    pre = _mlstm_pre(xm.reshape(b, s, inner), tail8, wts["a_prep"], cfg["ts_pre"], cfg["chunk"] is not None)
    if cfg["chunk"] is not None:
        xc, q, k, v, g, gt, conv8 = pre
        gcol = g.reshape(b, s, 2, heads_a).transpose(0, 3, 1, 2)
        grow = gt.reshape(b, 2, heads_a, s).transpose(0, 2, 1, 3)
        hh, c_new, n_new, m_new = _mlstm_scan(q, k, v, gcol, grow, cfg["chunk"])
        n_new = n_new.reshape(b, heads_a, hd)
        m_new = m_new.reshape(b, heads_a)
    else:
        xc, q, k, v, g, conv8 = pre
        c0, n0, mm0 = state
        g4 = g.reshape(b, 2, heads_a).transpose(0, 2, 1).reshape(b, heads_a, 1, 2)
        hh, c_new, n_new, m_new = _mlstm_step(q, k, v, g4, c0, n0.reshape(b, heads_a, 1, hd),
                                              mm0.reshape(b, heads_a, 1, 1))
        n_new = n_new.reshape(b, heads_a, hd)
        m_new = m_new.reshape(b, heads_a)
    conv_new = conv8[:, 8 - (A_CONV - 1):, :]
    rs = lambda a: a.reshape(row_shape + (a.shape[-1],))
    x1, h2p, ri = _mixer_out(
        functools.partial(_mixa_body, heads=heads_a), [rs(hh), rs(xc), z], xr, m0,
        [wts["a_norm_w"], wts["a_skip"], wts["a_w_out"], wts["ln_g"][0][0], wts["ln_b"][0][0], *wts["router"][0]],
        cfg["ts_out"], cfg["packed"], "mix_a")

    def moe(h2p, ri, xres, m, layer):
        lng, lnb = wts["ln_g"][layer][1], wts["ln_b"][layer][1]
        if cfg["dense_moe"]:
            return _moe_dense(h2p, ri, xres, m, *wts["moe"][layer], lng, lnb)
        return _moe(h2p, ri, xres, m, *wts["moe"][layer], lng, lnb, cfg["nbt"], cfg["tmx"], cfg["tc"])

    x2 = moe(h2p, ri, x1, m0, 0)

    k_sh, v_sh = _linear(x2, wts["kv_w"], ts, (F32, F32), tn=tn, name="kv_proj")

    m1 = mod[1]
    lam_init = 0.8 - 0.6 * math.exp(-0.3 * 1)
    (qq,) = _linear(x2, wts["b_w_q"], ts, (cfg["q_dtype"],), mod=m1, cols=(1, 0), out_scale=wts["q_scale"],
                    tn=tn, name="q_proj")
    if cache is None:
        o = _attn_prompt(qq.reshape(b, s, d), k_sh.reshape(b, s, d), v_sh.reshape(b, s, d),
                         wts["b_lambda"], wts["slopes"], lam_init, cfg["tq"], cfg["head_group"])
    else:
        cache_k, cache_v, page_table = cache
        hv = cache_k.shape[2:]
        o = _attn_decode(qq.reshape((b, s) + hv), k_sh.reshape((b, s) + hv), v_sh.reshape((b, s) + hv),
                         cache_k, cache_v, page_table, wts["b_lambda"], wts["slopes"], lam_init).reshape(b, s, d)
    heads_b = wts["slopes"].shape[0]
    x3, h2p, ri = _mixer_out(
        functools.partial(_mixb_body, heads=heads_b, gain=1.0 - lam_init), [rs(o)], x2, m1,
        [wts["b_subln_w"], wts["b_w_out"], wts["ln_g"][1][0], wts["ln_b"][1][0], *wts["router"][1]],
        cfg["ts_out"], cfg["packed"], "mix_b")
    y = moe(h2p, ri, x3, m1, 1)
    return (y.reshape(b, s, d), k_sh.reshape(b, s, d), v_sh.reshape(b, s, d), conv_new, c_new, n_new, m_new)


def kernel(x_prompt, x_sample, cache_k, cache_v, state_conv, state_C, state_n, state_m, page_table, c_prompt, c_sample, a_w_in, a_conv_w, a_conv_b, a_w_q, a_w_k, a_w_v, a_w_gate, a_b_gate, a_norm_w, a_skip, a_w_out, kv_w_k, kv_w_v, b_w_q, b_lambda, b_subln_w, b_w_out, moe_w_group, moe_b_group, moe_w_expert, moe_b_expert, moe_w1, moe_w3, moe_w2, ln_g, ln_b, ada_w, ada_b):
    bp, sp, d = x_prompt.shape
    db, ds, _ = x_sample.shape
    assert ds == 1
    heads_a = state_C.shape[2]
    heads_b = cache_k.shape[2]
    inner = a_conv_w.shape[2]
    hd_b = d // (2 * heads_b)
    ff = moe_w1.shape[-1]

    slopes = jnp.exp2(-8.0 * jnp.arange(1, heads_b + 1, dtype=F32) / heads_b)
    common = dict(
        heads_a=heads_a,
        a_norm_w=a_norm_w[0].reshape(1, inner), a_skip=a_skip[0].reshape(1, inner),
        q_scale=hd_b ** -0.5, b_lambda=b_lambda[0], b_subln_w=b_subln_w[0].reshape(1, -1), slopes=slopes,
        router=[_prep_router(moe_w_group[i], moe_b_group[i], moe_w_expert[i], moe_b_expert[i]) for i in range(DEPTH)],
        moe=[(moe_w1[i].reshape(N_EXPERTS, d, ff), moe_w3[i].reshape(N_EXPERTS, d, ff),
              moe_w2[i].reshape(N_EXPERTS, ff, d)) for i in range(DEPTH)],
        ln_g=[[ln_g[i, j].reshape(1, d) for j in range(2)] for i in range(DEPTH)],
        ln_b=[[ln_b[i, j].reshape(1, d) for j in range(2)] for i in range(DEPTH)],
    )

    def weights(dt):
        return dict(
            common,
            a_w_in=a_w_in[0].astype(dt),
            a_prep=_prep_mlstm(a_conv_w[0], a_conv_b[0], a_w_q[0], a_w_k[0], a_w_v[0], a_w_gate[0], a_b_gate[0],
                               heads_a, dt),
            a_w_out=a_w_out[0].astype(dt),
            kv_w=jnp.concatenate([kv_w_k, kv_w_v], axis=1).astype(dt),
            b_w_q=b_w_q[0].astype(dt), b_w_out=b_w_out[0].astype(dt))

    mod = _ada(jnp.concatenate([c_prompt, c_sample], axis=0), ada_w, ada_b)
    mod_p = mod[:, :bp].reshape(DEPTH, bp, 1, 6 * d)
    mod_s = mod[:, bp:].reshape(DEPTH, 1, db, 6 * d)

    cfg_p = dict(ts=512, ts_pre=256, ts_out=256, chunk=256, row_shape=(bp, sp), nbt=sp, tmx=128, tc=512, tq=512,
                 head_group=1, tn=None, packed=True, dense_moe=False, q_dtype=BF16)
    zero_conv = jnp.zeros((bp, A_CONV - 1, inner), F32)
    outs_p = _trunk(x_prompt, mod_p, zero_conv, None, None, weights(BF16), cfg_p)

    cfg_s = dict(ts=db, ts_pre=1, ts_out=db, chunk=None, row_shape=(1, db), tn=d, packed=False, dense_moe=True,
                 q_dtype=F32)
    cache = (cache_k, cache_v, page_table)
    outs_s = _trunk(x_sample, mod_s, state_conv[0], (state_C[0], state_n[0], state_m[0]), cache, weights(F32), cfg_s)

    def pack(o, nb_, s_):
        y, k, v, conv, c, n, m = o
        return (y, k.reshape(nb_, s_, heads_b, 2 * hd_b), v.reshape(nb_, s_, heads_b, 2 * hd_b),
                conv[None], c[None], n[None], m[None])

    yp, kp, vp, convp, cp, np_, mp = pack(outs_p, bp, sp)
    ys, ks, vs, convs, cs, ns, ms_ = pack(outs_s, db, ds)
    return (yp, ys, kp, vp, convp, cp, np_, mp, ks, vs, convs, cs, ns, ms_)
```

```python
import functools
import math

import jax
import jax.numpy as jnp
from jax import lax
from jax.experimental import pallas as pl
from jax.experimental.pallas import tpu as pltpu

F32 = jnp.float32
BF16 = jnp.bfloat16
I32 = jnp.int32
U32 = jnp.uint32

DEPTH = 2
LN_EPS = 1e-5
DN_ALPHA = (2.0 * DEPTH) ** 0.25
A_CONV = 4
A_QKV_BLOCK = 4
MOE_GROUPS = 4
MOE_EXPERTS = 4
N_EXPERTS = MOE_GROUPS * MOE_EXPERTS
PAGE_SIZE = 128
LANES = 128
PAGES_PER_STEP = 8
MIB = 1024 * 1024


def _cparams(sem, vmem_mib=48):
    return pltpu.CompilerParams(dimension_semantics=sem, vmem_limit_bytes=vmem_mib * MIB)


def _silu(x):
    return x / (1.0 + jnp.exp(-x))


def _log_sigmoid(x):
    return jnp.minimum(x, 0.0) - jnp.log(1.0 + jnp.exp(-jnp.abs(x)))


def _dot(a, b):
    return jnp.dot(a, b, preferred_element_type=F32)


def _dot_nt(a, b):
    return lax.dot_general(a, b, (((1,), (1,)), ((), ())), preferred_element_type=F32)


def _dot_tn(a, b):
    return lax.dot_general(a, b, (((0,), (0,)), ((), ())), preferred_element_type=F32)


def _split(x):
    hi = x.astype(BF16)
    return hi, (x - hi.astype(F32)).astype(BF16)


def _dot3(a, w, dot=None):
    dot = dot or _dot
    ah, al = _split(a)
    wh, wl = _split(w)
    return dot(ah, wh) + dot(al, wh) + dot(ah, wl)


def _mm(a, w):
    if w.dtype == F32:
        return _dot3(a.astype(F32), w)
    return _dot(a.astype(BF16), w)


def _bits(x):
    return lax.bitcast_convert_type(x, U32)


def _pack_pairs(y):
    w = y.shape[-1] // 2
    r = _bits(y.astype(BF16).astype(F32))
    return (r[:, w:] & jnp.uint32(0xFFFF0000)) | (r[:, :w] >> 16)


def _unpack_pairs(u):
    lo = lax.bitcast_convert_type(u << 16, F32)
    hi = lax.bitcast_convert_type(u & jnp.uint32(0xFFFF0000), F32)
    return lo, hi


def _layer_norm(y, g, b):
    mu = jnp.mean(y, axis=-1, keepdims=True)
    yc = y - mu
    var = jnp.mean(yc * yc, axis=-1, keepdims=True)
    return yc * lax.rsqrt(var + LN_EPS) * g + b


def _ada_body(c_ref, w_ref, b_ref, o_ref):
    s = _silu(c_ref[...])
    o_ref[0] = _dot3(s, w_ref[0]) + b_ref[0]


def _ada(c_all, ada_w, ada_b):
    r, d = c_all.shape
    depth, _, n = ada_w.shape
    tn = 1536
    return pl.pallas_call(
        _ada_body,
        grid=(depth, n // tn),
        in_specs=[pl.BlockSpec((r, d), lambda i, j: (0, 0)),
                  pl.BlockSpec((1, d, tn), lambda i, j: (i, 0, j)),
                  pl.BlockSpec((1, 1, tn), lambda i, j: (i, 0, j))],
        out_specs=pl.BlockSpec((1, r, tn), lambda i, j: (i, 0, j)),
        out_shape=jax.ShapeDtypeStruct((depth, r, n), F32),
        compiler_params=_cparams(("parallel", "parallel")),
        name="ada",
    )(c_all, ada_w, ada_b.reshape(depth, 1, n))


def _mod_spec(mod, ts, col):
    d = mod.shape[-1] // 6
    if mod.shape[1] == 1:
        return pl.BlockSpec((1, 1, d), lambda b, s, *_: (b, 0, col))
    return pl.BlockSpec((1, ts, d), lambda b, s, *_: (b, s, col))


def _lin_body(*refs, has_mod, out_scale):
    if has_mod:
        x_ref, sc_ref, sh_ref, w_ref, *o_refs = refs
        x = x_ref[0] * (1.0 + sc_ref[0]) + sh_ref[0]
    else:
        x_ref, w_ref, *o_refs = refs
        x = x_ref[0]
    y = _mm(x, w_ref[...])
    if out_scale != 1.0:
        y = y * out_scale
    n = y.shape[-1] // len(o_refs)
    for j, o_ref in enumerate(o_refs):
        o_ref[0] = y[:, j * n:(j + 1) * n].astype(o_ref.dtype)


def _linear(x, w, ts, out_dtypes, mod=None, cols=None, out_scale=1.0, tn=None, name="linear"):
    b, s, k = x.shape
    n = w.shape[1]
    no = n // len(out_dtypes)
    tn = tn or n
    assert n % tn == 0 and (tn == n or no % tn == 0)
    in_specs = [pl.BlockSpec((1, ts, k), lambda i, j, c: (i, j, 0))]
    args = [x]
    if mod is not None:
        in_specs += [_mod_spec(mod, ts, cols[0]), _mod_spec(mod, ts, cols[1])]
        args += [mod, mod]
    in_specs.append(pl.BlockSpec((k, tn), lambda i, j, c: (0, c)))
    args.append(w)
    if tn == n:
        out_specs = [pl.BlockSpec((1, ts, no), lambda i, j, c: (i, j, 0)) for _ in out_dtypes]
        out_shape = [jax.ShapeDtypeStruct((b, s, no), dt) for dt in out_dtypes]
    else:
        assert len(set(out_dtypes)) == 1
        out_specs = [pl.BlockSpec((1, ts, tn), lambda i, j, c: (i, j, c))]
        out_shape = [jax.ShapeDtypeStruct((b, s, n), out_dtypes[0])]
    outs = pl.pallas_call(
        functools.partial(_lin_body, has_mod=mod is not None, out_scale=out_scale),
        grid=(b, s // ts, n // tn),
        in_specs=in_specs,
        out_specs=out_specs,
        out_shape=out_shape,
        compiler_params=_cparams(("parallel", "parallel", "parallel"), 56),
        name=name,
    )(*args)
    if tn == n:
        return outs
    return [outs[0][:, :, g * no:(g + 1) * no] for g in range(len(out_dtypes))]


def _p1_body(xm_ref, tail_ref, cw_ref, cb_ref, wqkv_ref, wg_ref, wgt_ref, bg_ref, bgt_ref,
             xc_ref, q_ref, k_ref, v_ref, g_ref, conv_ref, xbuf, *, gates_on_lanes):
    s = pl.program_id(1)
    ts = xm_ref.shape[1]
    inner = xm_ref.shape[2]

    @pl.when(s == 0)
    def _():
        xbuf[0:8, :] = tail_ref[0]

    if ts >= 8:
        @pl.when(s > 0)
        def _():
            xbuf[0:8, :] = xbuf[ts:ts + 8, :]

    xm = xm_ref[0]
    xbuf[8:8 + ts, :] = xm
    conv = cb_ref[...] + xm * cw_ref[A_CONV - 1:A_CONV, :]
    for j in range(A_CONV - 1):
        conv = conv + xbuf[5 + j:5 + j + ts, :] * cw_ref[j:j + 1, :]
    conv_ref[0] = xbuf[ts:ts + 8, :]
    xc = _silu(conv)
    xc_ref[0] = xc
    cw = wqkv_ref.shape[2]
    for c in range(inner // cw):
        sl = slice(c * cw, (c + 1) * cw)
        q_ref[0, :, sl] = _mm(xc[:, sl], wqkv_ref[0, c]).astype(q_ref.dtype)
        k_ref[0, :, sl] = _mm(xc[:, sl], wqkv_ref[1, c]).astype(k_ref.dtype)
        v_ref[0, :, sl] = _mm(xm[:, sl], wqkv_ref[2, c]).astype(v_ref.dtype)
    q = q_ref[0]
    k = k_ref[0]
    v = v_ref[0]
    half = wgt_ref.shape[0] // 2
    if gates_on_lanes:
        gt = (_dot_nt(wgt_ref[:, 0:inner], q) + _dot_nt(wgt_ref[:, inner:2 * inner], k)
              + _dot_nt(wgt_ref[:, 2 * inner:3 * inner], v) + bgt_ref[...])
        sub = lax.broadcasted_iota(I32, gt.shape, 0)
        g_ref[0] = jnp.where(sub >= half, _log_sigmoid(gt), gt)
    else:
        g = (_mm(q, wg_ref[0:inner, :]) + _mm(k, wg_ref[inner:2 * inner, :])
             + _mm(v, wg_ref[2 * inner:3 * inner, :]) + bg_ref[...])
        lane = lax.broadcasted_iota(I32, g.shape, 1)
        g_ref[0] = jnp.where(lane >= half, _log_sigmoid(g), g)[:, 0:8]


def _mlstm_pre(xm, tail8, prep, ts, gates_on_lanes):
    b, s, inner = xm.shape
    assert s == ts or ts >= 8
    full = lambda shape: pl.BlockSpec(shape, lambda i, j: (0,) * len(shape))
    row = lambda n, dt: (pl.BlockSpec((1, ts, n), lambda i, j: (i, j, 0)), jax.ShapeDtypeStruct((b, s, n), dt))
    qdt = BF16 if prep["wqkv"].dtype == BF16 else F32
    outs = [row(inner, F32), row(inner, qdt), row(inner, qdt), row(inner, qdt)]
    if gates_on_lanes:
        outs.append((pl.BlockSpec((1, 8, ts), lambda i, j: (i, 0, j)), jax.ShapeDtypeStruct((b, 8, s), F32)))
    else:
        outs.append(row(8, F32))
    outs.append((pl.BlockSpec((1, 8, inner), lambda i, j: (i, 0, 0)), jax.ShapeDtypeStruct((b, 8, inner), F32)))
    return pl.pallas_call(
        functools.partial(_p1_body, gates_on_lanes=gates_on_lanes),
        grid=(b, s // ts),
        in_specs=[pl.BlockSpec((1, ts, inner), lambda i, j: (i, j, 0)),
                  pl.BlockSpec((1, 8, inner), lambda i, j: (i, 0, 0)),
                  full((A_CONV, inner)), full((1, inner)),
                  full(prep["wqkv"].shape),
                  full((3 * inner, LANES)), full((8, 3 * inner)), full((1, LANES)), full((8, 1))],
        out_specs=[o[0] for o in outs],
        out_shape=[o[1] for o in outs],
        scratch_shapes=[pltpu.VMEM((ts + 8, inner), F32)],
        compiler_params=_cparams(("parallel", "arbitrary"), 56),
        name="mlstm_pre",
    )(xm, tail8, prep["conv_w"], prep["conv_b"], prep["wqkv"],
      prep["wg"], prep["wgt"], prep["bg"], prep["bgt"])


def _p2_body(q_ref, k_ref, v_ref, gr_ref, h_ref, c_ref, n_ref, m_ref, caug, ms):
    c = pl.program_id(1)
    nc = pl.num_programs(1)
    L = q_ref.shape[1]
    heads = gr_ref.shape[1]
    hd = q_ref.shape[2] // heads

    @pl.when(c == 0)
    def _():
        caug[...] = jnp.zeros_like(caug)
        ms[...] = jnp.zeros_like(ms)

    row = lax.broadcasted_iota(I32, (L, L), 0)
    col = lax.broadcasted_iota(I32, (L, L), 1)
    causal = col <= row
    diag = col == row
    lane = lax.broadcasted_iota(I32, (L, LANES), 1)
    ones_col = jnp.where(lane == 0, 1.0, 0.0).astype(BF16)
    for a in range(heads):
        sl = slice(a * hd, (a + 1) * hd)
        q = q_ref[0, :, sl]
        k = k_ref[0, :, sl]
        v = v_ref[0, :, sl]
        gr = gr_ref[0, a]
        ig_r, lf_r = gr[0:1, :], gr[1:2, :]
        ig_c = jnp.sum(jnp.where(diag, ig_r, 0.0), axis=1, keepdims=True)
        lf_c = jnp.sum(jnp.where(diag, lf_r, 0.0), axis=1, keepdims=True)
        b_c = jnp.sum(jnp.where(causal, lf_r, 0.0), axis=1, keepdims=True)
        b_r = jnp.sum(jnp.where(row <= col, lf_c, 0.0), axis=0, keepdims=True)
        log_d = jnp.where(causal, b_c - b_r + ig_r, -jnp.inf)
        m_prev = ms[a]
        log_inter = b_c + m_prev
        m_t = jnp.maximum(log_inter, jnp.max(log_d, axis=1, keepdims=True))
        d = jnp.exp(log_d - m_t)
        w_inter = jnp.exp(log_inter - m_t)
        s = (_dot_nt(q, k) * d).astype(BF16)
        vaug = jnp.concatenate([v, ones_col], axis=1)
        ca = caug[a]
        num = w_inter * _dot(q, ca.astype(BF16)) + _dot(s, vaug)
        den = num[:, hd:hd + 1]
        h_ref[0, :, sl] = num[:, 0:hd] / jnp.maximum(jnp.abs(den), jnp.exp(-m_t))
        m_new = m_t[L - 1:L, :]
        b_last = b_c[L - 1:L, :]
        w_s = jnp.exp(b_last - b_c + ig_c - m_new)
        decay = jnp.exp(b_last + m_prev - m_new)
        kw = (k.astype(F32) * w_s).astype(BF16)
        caug[a] = decay * ca + _dot_tn(kw, vaug)
        ms[a] = m_new

    @pl.when(c == nc - 1)
    def _():
        for a in range(heads):
            c_ref[0, a] = caug[a, :, 0:hd]
            n_ref[0, a] = caug[a, :, hd:hd + 1]
            m_ref[0, a] = ms[a]


def _mlstm_scan(q, k, v, grow, L):
    b, s, inner = q.shape
    heads = grow.shape[1]
    hd = inner // heads
    qkv = pl.BlockSpec((1, L, inner), lambda i, c: (i, c, 0))
    return pl.pallas_call(
        _p2_body,
        grid=(b, s // L),
        in_specs=[qkv, qkv, qkv,
                  pl.BlockSpec((1, heads, 2, L), lambda i, c: (i, 0, 0, c))],
        out_specs=[qkv,
                   pl.BlockSpec((1, heads, hd, hd), lambda i, c: (i, 0, 0, 0)),
                   pl.BlockSpec((1, heads, hd, 1), lambda i, c: (i, 0, 0, 0)),
                   pl.BlockSpec((1, heads, 1, 1), lambda i, c: (i, 0, 0, 0))],
        out_shape=[jax.ShapeDtypeStruct((b, s, inner), F32),
                   jax.ShapeDtypeStruct((b, heads, hd, hd), F32),
                   jax.ShapeDtypeStruct((b, heads, hd, 1), F32),
                   jax.ShapeDtypeStruct((b, heads, 1, 1), F32)],
        scratch_shapes=[pltpu.VMEM((heads, hd, hd + LANES), F32), pltpu.VMEM((heads, 1, 1), F32)],
        compiler_params=_cparams(("parallel", "arbitrary")),
        name="mlstm_scan",
    )(q, k, v, grow)


def _s2_body(q_ref, k_ref, v_ref, g_ref, c_ref, n_ref, m_ref, h_ref, co_ref, no_ref, mo_ref):
    heads = c_ref.shape[1]
    hd = c_ref.shape[2]
    sub = lax.broadcasted_iota(I32, (8, hd), 0)
    first = sub == 0
    for a in range(heads):
        sl = slice(a * hd, (a + 1) * hd)
        q = q_ref[0, :, sl]
        k = k_ref[0, :, sl]
        v = v_ref[0, :, sl]
        g = g_ref[0, a]
        ig, lf = g[:, 0:1], g[:, 1:2]
        m_prev = m_ref[0, a]
        m_new = jnp.maximum(lf + m_prev, ig)
        decay = jnp.exp(lf + m_prev - m_new)
        dd = jnp.exp(ig - m_new)
        k8 = jnp.where(first, k, 0.0)
        v8 = jnp.where(first, v, 0.0)
        q8 = jnp.where(first, q, 0.0)
        c_new = decay * c_ref[0, a] + dd * _dot3(k8, v8, _dot_tn)
        n_new = decay * n_ref[0, a] + dd * k
        num = _dot3(q8, c_new)[0:1, :]
        den = jnp.sum(q * n_new, axis=1, keepdims=True)
        h_ref[0, :, sl] = num / jnp.maximum(jnp.abs(den), jnp.exp(-m_new))
        co_ref[0, a] = c_new
        no_ref[0, a] = n_new
        mo_ref[0, a] = m_new


def _mlstm_step(q, k, v, g4, c0, n0, m0):
    b, _, inner = q.shape
    heads = c0.shape[1]
    hd = inner // heads
    qkv = pl.BlockSpec((1, 1, inner), lambda i: (i, 0, 0))
    st = lambda r, c: pl.BlockSpec((1, heads, r, c), lambda i: (i, 0, 0, 0))
    return pl.pallas_call(
        _s2_body,
        grid=(b,),
        in_specs=[qkv, qkv, qkv, st(1, 2), st(hd, hd), st(1, hd), st(1, 1)],
        out_specs=[qkv, st(hd, hd), st(1, hd), st(1, 1)],
        out_shape=[jax.ShapeDtypeStruct((b, 1, inner), F32),
                   jax.ShapeDtypeStruct((b, heads, hd, hd), F32),
                   jax.ShapeDtypeStruct((b, heads, 1, hd), F32),
                   jax.ShapeDtypeStruct((b, heads, 1, 1), F32)],
        compiler_params=_cparams(("parallel",)),
        name="mlstm_step",
    )(q, k, v, g4, c0, n0, m0)


def _route(logits):
    ts = logits.shape[0]
    lane = lax.broadcasted_iota(I32, (ts, LANES), 1)
    neg = -jnp.inf
    big = jnp.int32(1 << 20)
    is_g = lane < MOE_GROUPS
    gl = jnp.where(is_g, logits, neg)
    gmax = jnp.max(gl, axis=1, keepdims=True)
    gidx = jnp.min(jnp.where(gl == gmax, lane, big), axis=1, keepdims=True)
    gsum = jnp.sum(jnp.where(is_g, jnp.exp(logits - gmax), 0.0), axis=1, keepdims=True)
    g_w = 1.0 / gsum
    lo = MOE_GROUPS + MOE_EXPERTS * gidx
    el = jnp.where(lane >= lo, jnp.where(lane < lo + MOE_EXPERTS, logits, neg), neg)
    v1 = jnp.max(el, axis=1, keepdims=True)
    i1 = jnp.min(jnp.where(el == v1, lane, big), axis=1, keepdims=True)
    el2 = jnp.where(lane == i1, neg, el)
    v2 = jnp.max(el2, axis=1, keepdims=True)
    i2 = jnp.min(jnp.where(el2 == v2, lane, big), axis=1, keepdims=True)
    t = jnp.exp(v2 - v1)
    w1 = g_w / (1.0 + t)
    w2 = g_w * t / (1.0 + t)
    e1 = (i1 - MOE_GROUPS).astype(F32)
    e2 = (i2 - MOE_GROUPS).astype(F32)
    return jnp.where(lane == 0, e1, jnp.where(lane == 1, e2, jnp.where(lane == 2, w1, jnp.where(lane == 3, w2, 0.0))))


def _epilogue(x, o, g1, sc2, sh2, lng, lnb, wrh, wrl, br, x1_ref, h2_ref, ri_ref):
    x1 = _layer_norm(DN_ALPHA * x + (1.0 + g1) * o, lng, lnb)
    x1_ref[0] = x1
    h2 = x1 * (1.0 + sc2) + sh2
    h2_ref[0] = _pack_pairs(h2) if h2_ref.dtype == U32 else h2
    xh = h2.astype(BF16)
    xl = (h2 - xh.astype(F32)).astype(BF16)
    logits = _dot(xh, wrh) + _dot(xl, wrh) + _dot(xh, wrl) + br
    ri_ref[0] = _route(logits)


def _mixa_body(h_ref, xc_ref, z_ref, x_ref, g1_ref, sc2_ref, sh2_ref, nw_ref, skip_ref, wo_ref,
               lng_ref, lnb_ref, wrh_ref, wrl_ref, br_ref, x1_ref, h2_ref, ri_ref, *, heads):
    h = h_ref[0]
    inner = h.shape[1]
    hd = inner // heads
    parts = []
    for a in range(heads):
        ha = h[:, a * hd:(a + 1) * hd]
        hc = ha - jnp.mean(ha, axis=1, keepdims=True)
        parts.append(hc * lax.rsqrt(jnp.mean(hc * hc, axis=1, keepdims=True) + LN_EPS))
    hn = jnp.concatenate(parts, axis=1) * nw_ref[...]
    z = z_ref[0]
    out = (hn + skip_ref[...] * xc_ref[0]) * (1.0 / (1.0 + jnp.exp(-z)))
    o = _mm(out, wo_ref[...])
    _epilogue(x_ref[0], o, g1_ref[0], sc2_ref[0], sh2_ref[0], lng_ref[...], lnb_ref[...],
              wrh_ref[...], wrl_ref[...], br_ref[...], x1_ref, h2_ref, ri_ref)


def _mixb_body(o_ref, x_ref, g1_ref, sc2_ref, sh2_ref, sw_ref, wo_ref,
               lng_ref, lnb_ref, wrh_ref, wrl_ref, br_ref, x1_ref, h2_ref, ri_ref, *, heads, gain):
    o = o_ref[0]
    vd = o.shape[1] // heads
    parts = []
    for a in range(heads):
        oa = o[:, a * vd:(a + 1) * vd]
        parts.append(oa * lax.rsqrt(jnp.mean(oa * oa, axis=1, keepdims=True) + LN_EPS) * sw_ref[...] * gain)
    on = jnp.concatenate(parts, axis=1)
    y = _mm(on, wo_ref[...])
    _epilogue(x_ref[0], y, g1_ref[0], sc2_ref[0], sh2_ref[0], lng_ref[...], lnb_ref[...],
              wrh_ref[...], wrl_ref[...], br_ref[...], x1_ref, h2_ref, ri_ref)


def _mixer_out(body, acts, x, mod, consts, ts, packed, name):
    b, s, d = x.shape
    hw, hdt = (d // 2, U32) if packed else (d, F32)
    rowspec = lambda a: pl.BlockSpec((1, ts, a.shape[2]), lambda i, j: (i, j, 0))
    full = lambda a: pl.BlockSpec(a.shape, lambda i, j: (0,) * a.ndim)
    in_specs = [rowspec(a) for a in acts] + [rowspec(x)]
    in_specs += [_mod_spec(mod, ts, 2), _mod_spec(mod, ts, 4), _mod_spec(mod, ts, 3)]
    in_specs += [full(c) for c in consts]
    return pl.pallas_call(
        body,
        grid=(b, s // ts),
        in_specs=in_specs,
        out_specs=[pl.BlockSpec((1, ts, d), lambda i, j: (i, j, 0)),
                   pl.BlockSpec((1, ts, hw), lambda i, j: (i, j, 0)),
                   pl.BlockSpec((1, ts, LANES), lambda i, j: (i, j, 0))],
        out_shape=[jax.ShapeDtypeStruct((b, s, d), F32),
                   jax.ShapeDtypeStruct((b, s, hw), hdt),
                   jax.ShapeDtypeStruct((b, s, LANES), F32)],
        compiler_params=_cparams(("parallel", "parallel"), 56),
        name=name,
    )(*acts, x, mod, mod, mod, *consts)


def _sort_body(e1_ref, e2_ref, pa_ref, pb_ref, cnt_ref, *, tmx, cw):
    nb_tok = e1_ref.shape[2]
    e1 = e1_ref[0]
    e2 = e2_ref[0]
    sub = lax.broadcasted_iota(I32, (N_EXPERTS, nb_tok), 0)
    a_hot = sub == e1
    b_hot = sub == e2
    m = jnp.where(a_hot, 1.0, 0.0) + jnp.where(b_hot, 1.0, 0.0)
    cnt = jnp.sum(m, axis=1, keepdims=True)
    padded = jnp.floor((cnt + (tmx - 1)) / tmx) * tmx
    r16 = lax.broadcasted_iota(I32, (N_EXPERTS, N_EXPERTS), 0)
    c16 = lax.broadcasted_iota(I32, (N_EXPERTS, N_EXPERTS), 1)
    prow = jnp.sum(jnp.where(r16 == c16, padded, 0.0), axis=0, keepdims=True)
    segoff = jnp.sum(jnp.where(c16 < r16, prow, 0.0), axis=1, keepdims=True)
    cnt_ref[0] = jnp.broadcast_to(cnt, (N_EXPERTS, LANES)).astype(I32)
    ur = lax.broadcasted_iota(I32, (cw, cw), 0)
    uc = lax.broadcasted_iota(I32, (cw, cw), 1)
    upper = jnp.where(ur < uc, 1.0, 0.0).astype(BF16)
    carry = segoff
    for j in range(nb_tok // cw):
        sl = slice(j * cw, (j + 1) * cw)
        mc = m[:, sl]
        rank = _dot(mc.astype(BF16), upper) + carry
        pa_ref[0, :, sl] = jnp.sum(jnp.where(a_hot[:, sl], rank, 0.0), axis=0, keepdims=True).astype(I32)
        pb_ref[0, :, sl] = jnp.sum(jnp.where(b_hot[:, sl], rank, 0.0), axis=0, keepdims=True).astype(I32)
        carry = carry + jnp.sum(mc, axis=1, keepdims=True)


def _moe_sort(e1, e2, tmx):
    nb, _, nbt = e1.shape
    cw = min(nbt, 256)
    tok = pl.BlockSpec((1, 1, nbt), lambda i: (i, 0, 0))
    return pl.pallas_call(
        functools.partial(_sort_body, tmx=tmx, cw=cw),
        grid=(nb,),
        in_specs=[tok, tok],
        out_specs=[tok, tok, pl.BlockSpec((1, N_EXPERTS, LANES), lambda i: (i, 0, 0))],
        out_shape=[jax.ShapeDtypeStruct((nb, 1, nbt), I32), jax.ShapeDtypeStruct((nb, 1, nbt), I32),
                   jax.ShapeDtypeStruct((nb, N_EXPERTS, LANES), I32)],
        compiler_params=_cparams(("parallel",)),
        name="moe_sort",
    )(e1, e2)


def _dispatch_body(pa_ref, pb_ref, src_ref, dst_ref):
    nbt = src_ref.shape[1]
    dst_ref[...] = jnp.zeros_like(dst_ref)

    def body(i, carry):
        row = src_ref[0, pl.ds(i, 1), :]
        dst_ref[0, pl.ds(pa_ref[0, 0, i], 1), :] = row
        dst_ref[0, pl.ds(pb_ref[0, 0, i], 1), :] = row
        return carry

    lax.fori_loop(0, nbt, body, 0, unroll=min(8, nbt))


def _moe_dispatch(pa, pb, h2p, rp):
    nb, nbt, w = h2p.shape
    smem = pl.BlockSpec((1, 1, nbt), lambda i: (i, 0, 0), memory_space=pltpu.SMEM)
    return pl.pallas_call(
        _dispatch_body,
        grid=(nb,),
        in_specs=[smem, smem, pl.BlockSpec((1, nbt, w), lambda i: (i, 0, 0))],
        out_specs=pl.BlockSpec((1, rp, w), lambda i: (i, 0, 0), pipeline_mode=pl.Buffered(1)),
        out_shape=jax.ShapeDtypeStruct((nb, rp, w), U32),
        compiler_params=_cparams(("arbitrary",), 56),
        name="moe_dispatch",
    )(pa, pb, h2p)


def _expert_body(tb_ref, tr_ref, te_ref, tv_ref, tf_ref, x_ref, w1_ref, w3_ref, w2_ref, o_ref, w1b, w3b, w2b):
    t = pl.program_id(0)

    @pl.when(tf_ref[t] == 1)
    def _():
        w1b[...] = w1_ref[0].astype(BF16)
        w3b[...] = w3_ref[0].astype(BF16)
        w2b[...] = w2_ref[0].astype(BF16)

    @pl.when(tv_ref[t] == 1)
    def _():
        lo, hi = _unpack_pairs(x_ref[0])
        lo = lo.astype(BF16)
        hi = hi.astype(BF16)
        half = lo.shape[1]
        a = _dot(lo, w1b[0:half, :]) + _dot(hi, w1b[half:2 * half, :])
        g = _dot(lo, w3b[0:half, :]) + _dot(hi, w3b[half:2 * half, :])
        hact = (_silu(a) * g).astype(BF16)
        o_ref[0] = _pack_pairs(_dot(hact, w2b[...]))


def _moe_experts(tables, xs, w1, w3, w2, tmx):
    nb, rp, half = xs.shape
    ne, d, ff = w1.shape
    nt = tables[0].shape[0]
    xspec = pl.BlockSpec((1, tmx, half), lambda t, tb, tr, te, tv, tf: (tb[t], tr[t], 0))
    grid_spec = pltpu.PrefetchScalarGridSpec(
        num_scalar_prefetch=5,
        grid=(nt,),
        in_specs=[xspec,
                  pl.BlockSpec((1, d, ff), lambda t, tb, tr, te, tv, tf: (te[t], 0, 0)),
                  pl.BlockSpec((1, d, ff), lambda t, tb, tr, te, tv, tf: (te[t], 0, 0)),
                  pl.BlockSpec((1, ff, d), lambda t, tb, tr, te, tv, tf: (te[t], 0, 0))],
        out_specs=xspec,
        scratch_shapes=[pltpu.VMEM((d, ff), BF16), pltpu.VMEM((d, ff), BF16), pltpu.VMEM((ff, d), BF16)],
    )
    return pl.pallas_call(
        _expert_body,
        grid_spec=grid_spec,
        out_shape=jax.ShapeDtypeStruct((nb, rp, half), U32),
        compiler_params=_cparams(("arbitrary",)),
        name="moe_experts",
    )(*tables, xs, w1, w3, w2)


def _combine_body(pa_ref, pb_ref, ri_ref, ys_ref, x1_ref, g2_ref, lng_ref, lnb_ref, o_ref, ga, gb):
    tc = x1_ref.shape[1]

    def body(i, carry):
        ga[pl.ds(i, 1), :] = ys_ref[0, pl.ds(pa_ref[0, 0, i], 1), :]
        gb[pl.ds(i, 1), :] = ys_ref[0, pl.ds(pb_ref[0, 0, i], 1), :]
        return carry

    lax.fori_loop(0, tc, body, 0, unroll=min(8, tc))
    alo, ahi = _unpack_pairs(ga[...])
    blo, bhi = _unpack_pairs(gb[...])
    ri = ri_ref[0]
    wa = ri[:, 2:3]
    wb = ri[:, 3:4]
    ffn = jnp.concatenate([wa * alo + wb * blo, wa * ahi + wb * bhi], axis=1)
    y = DN_ALPHA * x1_ref[0] + (1.0 + g2_ref[0]) * ffn
    o_ref[0] = _layer_norm(y, lng_ref[...], lnb_ref[...])


def _moe_combine(pa, pb, ri, ys, x1, mod, lng, lnb, tc):
    nb, nbt, d = x1.shape
    rp, half = ys.shape[1:]
    smem = pl.BlockSpec((1, 1, tc), lambda i, j: (i, 0, j), memory_space=pltpu.SMEM)
    per_mod = (nb * nbt) // mod.shape[0]
    assert mod.shape[1] == 1 and per_mod % tc == 0
    return pl.pallas_call(
        _combine_body,
        grid=(nb, nbt // tc),
        in_specs=[smem, smem,
                  pl.BlockSpec((1, tc, LANES), lambda i, j: (i, j, 0)),
                  pl.BlockSpec((1, rp, half), lambda i, j: (i, 0, 0), pipeline_mode=pl.Buffered(1)),
                  pl.BlockSpec((1, tc, d), lambda i, j: (i, j, 0)),
                  pl.BlockSpec((1, 1, d), lambda i, j: ((i * nbt + j * tc) // per_mod, 0, 5)),
                  pl.BlockSpec((1, d), lambda i, j: (0, 0)),
                  pl.BlockSpec((1, d), lambda i, j: (0, 0))],
        out_specs=pl.BlockSpec((1, tc, d), lambda i, j: (i, j, 0)),
        out_shape=jax.ShapeDtypeStruct((nb, nbt, d), F32),
        scratch_shapes=[pltpu.VMEM((tc, half), U32), pltpu.VMEM((tc, half), U32)],
        compiler_params=_cparams(("parallel", "arbitrary"), 56),
        name="moe_combine",
    )(pa, pb, ri, ys, x1, mod, lng, lnb)


def _tile_tables(cnt, tmx, max_tiles):
    nb = cnt.shape[0]
    nt = (cnt + (tmx - 1)) // tmx
    first_tile = jnp.cumsum(nt, axis=1) - nt
    nt_eb = nt.T.reshape(-1)
    ends = jnp.cumsum(nt_eb)
    total = ends[-1]
    t = jnp.arange(max_tiles, dtype=I32)
    valid = t < total
    tc = jnp.minimum(t, total - 1)
    seg = jnp.sum((ends[None, :] <= tc[:, None]).astype(I32), axis=1)
    within = tc - (ends[seg] - nt_eb[seg])
    te = seg // nb
    tb = seg % nb
    tr = first_tile[tb, te] + within
    prev = jnp.concatenate([jnp.full((1,), -1, I32), te[:-1]])
    tf = (valid & (te != prev)).astype(I32)
    return tb.astype(I32), tr.astype(I32), te.astype(I32), valid.astype(I32), tf


def _moe(h2p, ri, x1, mod, w1, w3, w2, lng, lnb, nbt, tmx, tc):
    b, s, d = x1.shape
    nb = (b * s) // nbt
    rp = 2 * nbt + N_EXPERTS * tmx
    max_tiles = nb * (2 * nbt // tmx + N_EXPERTS)
    r = ri.reshape(nb, nbt, LANES)
    e1 = r[:, :, 0].astype(I32).reshape(nb, 1, nbt)
    e2 = r[:, :, 1].astype(I32).reshape(nb, 1, nbt)
    pa, pb, cnt = _moe_sort(e1, e2, tmx)
    tables = _tile_tables(cnt[:, :, 0], tmx, max_tiles)
    xs = _moe_dispatch(pa, pb, h2p.reshape(nb, nbt, d // 2), rp)
    ys = _moe_experts(tables, xs, w1, w3, w2, tmx)
    x2 = _moe_combine(pa, pb, r, ys, x1.reshape(nb, nbt, d), mod, lng, lnb, tc)
    return x2.reshape(b, s, d)


def _moe_dense_body(h_ref, ri_ref, x1_ref, g2_ref, lng_ref, lnb_ref, w1_ref, w3_ref, w2_ref, o_ref, acc):
    e = pl.program_id(0)

    @pl.when(e == 0)
    def _():
        acc[...] = jnp.zeros_like(acc)

    x = h_ref[0]
    ri = ri_ref[0]
    ef = e.astype(F32)
    comb = jnp.where(ri[:, 0:1] == ef, ri[:, 2:3], 0.0) + jnp.where(ri[:, 1:2] == ef, ri[:, 3:4], 0.0)
    hact = _silu(_dot3(x, w1_ref[0])) * _dot3(x, w3_ref[0]) * comb
    acc[...] += _dot3(hact, w2_ref[0])

    @pl.when(e == pl.num_programs(0) - 1)
    def _():
        y = DN_ALPHA * x1_ref[0] + (1.0 + g2_ref[0]) * acc[...]
        o_ref[0] = _layer_norm(y, lng_ref[...], lnb_ref[...])


def _moe_dense(h2, ri, x1, mod, w1, w3, w2, lng, lnb):
    _, r, d = x1.shape
    ne, _, ff = w1.shape
    row = lambda n: pl.BlockSpec((1, r, n), lambda e: (0, 0, 0))
    return pl.pallas_call(
        _moe_dense_body,
        grid=(ne,),
        in_specs=[row(d), row(LANES), row(d),
                  pl.BlockSpec((1, r, d), lambda e: (0, 0, 5)),
                  pl.BlockSpec((1, d), lambda e: (0, 0)), pl.BlockSpec((1, d), lambda e: (0, 0)),
                  pl.BlockSpec((1, d, ff), lambda e: (e, 0, 0)), pl.BlockSpec((1, d, ff), lambda e: (e, 0, 0)),
                  pl.BlockSpec((1, ff, d), lambda e: (e, 0, 0))],
        out_specs=row(d),
        out_shape=jax.ShapeDtypeStruct((1, r, d), F32),
        scratch_shapes=[pltpu.VMEM((r, d), F32)],
        compiler_params=_cparams(("arbitrary",)),
        name="moe_dense",
    )(h2, ri, x1, mod, lng, lnb, w1, w3, w2)


def _lambda(lam_ref):
    lp = lam_ref[...]
    a = jnp.sum(lp[0:1, :] * lp[1:2, :], axis=1, keepdims=True)
    c = jnp.sum(lp[2:3, :] * lp[3:4, :], axis=1, keepdims=True)
    return jnp.exp(a) - jnp.exp(c)


POS_SPLIT = 16


ONES_ROWS = 16


def _attn_body(slopes_ref, q_ref, k_ref, v_ref, lam_ref, o_ref, kaug, vt, qaug, m_s, acc, *, lam_init, vd):
    hg = pl.program_id(1)
    j = pl.program_id(2)
    tq = q_ref.shape[1]
    group = q_ref.shape[2] // vd
    s_len = k_ref.shape[1]
    tk = tq

    @pl.when(j == 0)
    def _():
        pos = lax.broadcasted_iota(I32, (s_len, LANES), 0)
        ln = lax.broadcasted_iota(I32, (s_len, LANES), 1)
        hi = (pos // POS_SPLIT).astype(F32)
        lo = (pos % POS_SPLIT).astype(F32)
        ktail = jnp.where(ln == 0, hi, jnp.where(ln == 1, lo, 0.0)).astype(BF16)
        orow = lax.broadcasted_iota(I32, (ONES_ROWS, s_len), 0)
        ones_rows = jnp.where(orow == 0, 1.0, 0.0).astype(BF16)
        for g in range(group):
            kaug[g, :, 0:vd] = k_ref[0, :, g * vd:(g + 1) * vd].astype(BF16)
            kaug[g, :, vd:vd + LANES] = ktail
            vt[g, 0:vd, :] = v_ref[0, :, g * vd:(g + 1) * vd].T.astype(BF16)
            vt[g, vd:vd + ONES_ROWS, :] = ones_rows

    lane = lax.broadcasted_iota(I32, (tq, vd), 1)
    for g in range(group):
        slope = slopes_ref[hg * group + g]
        q = q_ref[0, :, g * vd:(g + 1) * vd].astype(F32)
        extra = jnp.where(lane == 0, slope * POS_SPLIT, jnp.where(lane == 1, slope, 0.0)).astype(BF16)
        qaug[g, 0:tq, 0:vd] = jnp.where(lane < vd // 2, q, 0.0).astype(BF16)
        qaug[g, 0:tq, vd:vd + LANES] = extra
        qaug[g, tq:2 * tq, 0:vd] = jnp.where(lane >= vd // 2, q, 0.0).astype(BF16)
        qaug[g, tq:2 * tq, vd:vd + LANES] = extra
    m_s[...] = jnp.full_like(m_s, -jnp.inf)
    acc[...] = jnp.zeros_like(acc)

    def block(kb, masked):
        off = pl.multiple_of(kb * tk, tk)
        for g in range(group):
            st = _dot_nt(kaug[g, pl.ds(off, tk), :], qaug[g])
            if masked:
                r = lax.broadcasted_iota(I32, st.shape, 0)
                c = lax.broadcasted_iota(I32, st.shape, 1)
                st = jnp.where(r <= (c & (tq - 1)), st, -jnp.inf)
            m_old = m_s[g]
            m_new = jnp.maximum(m_old, jnp.max(st, axis=0, keepdims=True))
            a = jnp.exp(m_old - m_new)
            p = jnp.exp(st - m_new)
            acc[g] = a * acc[g] + _dot(vt[g, :, pl.ds(off, tk)], p.astype(BF16))
            m_s[g] = m_new

    def off_diagonal(kb, carry):
        block(kb, False)
        return carry

    lax.fori_loop(0, j, off_diagonal, 0)
    block(j, True)
    lam = _lambda(lam_ref) + lam_init
    for g in range(group):
        num = acc[g, 0:vd, :]
        den = acc[g, vd:vd + 1, :]
        ot = num[:, 0:tq] / den[:, 0:tq] - lam * (num[:, tq:2 * tq] / den[:, tq:2 * tq])
        o_ref[0, :, g * vd:(g + 1) * vd] = ot.T


def _attn_prompt(q, k, v, lam_p, slopes, lam_init, tq, group):
    b, s, d = q.shape
    heads = slopes.shape[0]
    vd = d // heads
    gw = group * vd
    assert heads % group == 0 and tq & (tq - 1) == 0
    assert s // POS_SPLIT <= 256
    grid_spec = pltpu.PrefetchScalarGridSpec(
        num_scalar_prefetch=1,
        grid=(b, heads // group, s // tq),
        in_specs=[pl.BlockSpec((1, tq, gw), lambda i, h, j, sl: (i, j, h)),
                  pl.BlockSpec((1, s, gw), lambda i, h, j, sl: (i, 0, h)),
                  pl.BlockSpec((1, s, gw), lambda i, h, j, sl: (i, 0, h)),
                  pl.BlockSpec(lam_p.shape, lambda i, h, j, sl: (0, 0))],
        out_specs=pl.BlockSpec((1, tq, gw), lambda i, h, j, sl: (i, j, h)),
        scratch_shapes=[pltpu.VMEM((group, s, vd + LANES), BF16), pltpu.VMEM((group, vd + ONES_ROWS, s), BF16),
                        pltpu.VMEM((group, 2 * tq, vd + LANES), BF16),
                        pltpu.VMEM((group, 1, 2 * tq), F32),
                        pltpu.VMEM((group, vd + ONES_ROWS, 2 * tq), F32)],
    )
    return pl.pallas_call(
        functools.partial(_attn_body, lam_init=lam_init, vd=vd),
        grid_spec=grid_spec,
        out_shape=jax.ShapeDtypeStruct((b, s, d), F32),
        compiler_params=_cparams(("parallel", "parallel", "arbitrary"), 56),
        name="attn_prompt",
    )(slopes, q, k, v, lam_p)


def _decode_body(pt_ref, q_ref, kn_ref, vn_ref, slope_ref, lam_ref, *rest, lam_init, heads):
    npg = PAGES_PER_STEP
    k_refs = rest[:npg]
    v_refs = rest[npg:2 * npg]
    o_ref, m_s, l_s, acc, mb = rest[2 * npg:]
    j = pl.program_id(1)
    nj = pl.num_programs(1)
    vd = q_ref.shape[3]
    rows = 2 * heads
    keys = PAGE_SIZE * heads
    step_keys = npg * PAGE_SIZE
    lane = lax.broadcasted_iota(I32, (heads, vd), 1)
    qf = q_ref[0, 0].astype(F32)
    qm = jnp.concatenate([jnp.where(lane < vd // 2, qf, 0.0), jnp.where(lane >= vd // 2, qf, 0.0)], axis=0)
    shift = slope_ref[...] * float(step_keys)

    @pl.when(j == 0)
    def _():
        kn = kn_ref[0, 0]
        vn = vn_ref[0, 0]
        s_self = jnp.sum(qm * jnp.concatenate([kn, kn], axis=0), axis=1, keepdims=True)
        m_s[...] = s_self + shift * (1.0 * nj)
        l_s[...] = jnp.ones_like(l_s)
        acc[...] = jnp.concatenate([vn, vn], axis=0)
        r_i = lax.broadcasted_iota(I32, mb.shape, 0)
        c_i = lax.broadcasted_iota(I32, mb.shape, 1)
        dist = (step_keys - c_i // heads).astype(F32)
        mb[...] = jnp.where((c_i % heads) == (r_i % heads), -slope_ref[...] * dist, -jnp.inf)

    qh, ql = _split(qm)
    qhl = jnp.concatenate([qh, ql], axis=0)

    def scores(i):
        both = _dot_nt(qhl, k_refs[i][0].reshape(keys, vd).astype(BF16))
        return both[0:rows] + both[rows:2 * rows]

    s = jnp.concatenate([scores(i) for i in range(npg)], axis=1) + mb[...]
    m_old = m_s[...] - shift
    m_new = jnp.maximum(m_old, jnp.max(s, axis=1, keepdims=True))
    a = jnp.exp(m_old - m_new)
    p = jnp.exp(s - m_new)
    l_s[...] = a * l_s[...] + jnp.sum(p, axis=1, keepdims=True)
    ph, plo = _split(p)
    phl = jnp.concatenate([ph, plo], axis=0)
    pv = jnp.zeros((rows, vd), F32)
    for i in range(npg):
        both = _dot(phl[:, i * keys:(i + 1) * keys], v_refs[i][0].reshape(keys, vd).astype(BF16))
        pv = pv + both[0:rows] + both[rows:2 * rows]
    acc[...] = a * acc[...] + pv
    m_s[...] = m_new

    @pl.when(j == nj - 1)
    def _():
        lam = _lambda(lam_ref) + lam_init
        o16 = acc[...] / l_s[...]
        o_ref[0, 0] = o16[0:heads, :] - lam * o16[heads:rows, :]


def _attn_decode(q, k_new, v_new, cache_k, cache_v, page_table, lam_p, slopes, lam_init):
    db, _, heads, vd = q.shape
    n_pages = page_table.shape[1]
    npg = PAGES_PER_STEP
    assert n_pages % npg == 0
    rows = 2 * heads
    slope16 = jnp.concatenate([slopes, slopes]).reshape(rows, 1)
    tok = pl.BlockSpec((1, 1, heads, vd), lambda i, j, pt: (i, 0, 0, 0))

    def page_spec(idx):
        return pl.BlockSpec((1, PAGE_SIZE, heads, vd), lambda i, j, pt: (pt[i, j * npg + idx], 0, 0, 0))

    grid_spec = pltpu.PrefetchScalarGridSpec(
        num_scalar_prefetch=1,
        grid=(db, n_pages // npg),
        in_specs=[tok, tok, tok,
                  pl.BlockSpec((rows, 1), lambda i, j, pt: (0, 0)),
                  pl.BlockSpec(lam_p.shape, lambda i, j, pt: (0, 0))]
                 + [page_spec(i) for i in range(npg)] + [page_spec(i) for i in range(npg)],
        out_specs=tok,
        scratch_shapes=[pltpu.VMEM((rows, 1), F32), pltpu.VMEM((rows, 1), F32), pltpu.VMEM((rows, vd), F32),
                        pltpu.VMEM((rows, npg * PAGE_SIZE * heads), F32)],
    )
    return pl.pallas_call(
        functools.partial(_decode_body, lam_init=lam_init, heads=heads),
        grid_spec=grid_spec,
        out_shape=jax.ShapeDtypeStruct((db, 1, heads, vd), F32),
        compiler_params=_cparams(("parallel", "arbitrary")),
        name="attn_decode",
    )(page_table, q, k_new, v_new, slope16, lam_p, *([cache_k] * npg), *([cache_v] * npg))


MXU_WIDTH = 256


def _headwise_blocks(w):
    per = MXU_WIDTH // A_QKV_BLOCK
    wc = w.reshape(-1, per, A_QKV_BLOCK, A_QKV_BLOCK)
    eye = jnp.eye(per, dtype=w.dtype)
    return jnp.einsum("cnij,nm->cnimj", wc, eye).reshape(-1, MXU_WIDTH, MXU_WIDTH)


def _prep_mlstm(conv_w, conv_b, w_q, w_k, w_v, w_gate, b_gate, heads, dt):
    inner = conv_w.shape[1]
    hd = inner // heads
    ng = b_gate.shape[0]
    wg = w_gate.reshape(heads, 3, hd, ng).transpose(1, 0, 2, 3).reshape(3 * inner, ng)
    wgp = jnp.pad(wg, ((0, 0), (0, LANES - ng)))
    return dict(conv_w=conv_w, conv_b=conv_b.reshape(1, inner),
                wqkv=jnp.stack([_headwise_blocks(w_q), _headwise_blocks(w_k) * hd ** -0.5,
                                _headwise_blocks(w_v)]).astype(dt),
                wg=wgp.astype(dt), wgt=wg.T.astype(BF16),
                bg=jnp.pad(b_gate, (0, LANES - ng)).reshape(1, LANES), bgt=b_gate.reshape(ng, 1))


def _prep_router(w_group, b_group, w_expert, b_expert):
    d = w_group.shape[0]
    we = w_expert.transpose(1, 0, 2).reshape(d, N_EXPERTS)
    wr = jnp.pad(jnp.concatenate([w_group, we], axis=1), ((0, 0), (0, LANES - MOE_GROUPS - N_EXPERTS)))
    br = jnp.pad(jnp.concatenate([b_group, b_expert.reshape(-1)]), (0, LANES - MOE_GROUPS - N_EXPERTS))
    wrh = wr.astype(BF16)
    wrl = (wr - wrh.astype(F32)).astype(BF16)
    return wrh, wrl, br.reshape(1, LANES)


def _trunk(x, mod, conv0, state, cache, wts, cfg):
    b, s, d = x.shape
    heads_a = wts["heads_a"]
    ts = cfg["ts"]
    row_shape = cfg["row_shape"]
    xr = x.reshape(row_shape + (d,))

    m0 = mod[0]
    tn = cfg["tn"]
    xm, z = _linear(xr, wts["a_w_in"], ts, (F32, F32), mod=m0, cols=(1, 0), tn=tn, name="a_in")
    inner = xm.shape[-1]
    hd = inner // heads_a
    tail8 = jnp.pad(conv0, ((0, 0), (8 - (A_CONV - 1), 0), (0, 0)))
    pre = _mlstm_pre(xm.reshape(b, s, inner), tail8, wts["a_prep"], cfg["ts_pre"], cfg["chunk"] is not None)
    if cfg["chunk"] is not None:
        xc, q, k, v, gt, conv8 = pre
        grow = gt.reshape(b, 2, heads_a, s).transpose(0, 2, 1, 3)
        hh, c_new, n_new, m_new = _mlstm_scan(q, k, v, grow, cfg["chunk"])
        n_new = n_new.reshape(b, heads_a, hd)
        m_new = m_new.reshape(b, heads_a)
    else:
        xc, q, k, v, g, conv8 = pre
        c0, n0, mm0 = state
        g4 = g.reshape(b, 2, heads_a).transpose(0, 2, 1).reshape(b, heads_a, 1, 2)
        hh, c_new, n_new, m_new = _mlstm_step(q, k, v, g4, c0, n0.reshape(b, heads_a, 1, hd),
                                              mm0.reshape(b, heads_a, 1, 1))
        n_new = n_new.reshape(b, heads_a, hd)
        m_new = m_new.reshape(b, heads_a)
    conv_new = conv8[:, 8 - (A_CONV - 1):, :]
    rs = lambda a: a.reshape(row_shape + (a.shape[-1],))
    x1, h2p, ri = _mixer_out(
        functools.partial(_mixa_body, heads=heads_a), [rs(hh), rs(xc), z], xr, m0,
        [wts["a_norm_w"], wts["a_skip"], wts["a_w_out"], wts["ln_g"][0][0], wts["ln_b"][0][0], *wts["router"][0]],
        cfg["ts_out"], cfg["packed"], "mix_a")

    def moe(h2p, ri, xres, m, layer):
        lng, lnb = wts["ln_g"][layer][1], wts["ln_b"][layer][1]
        if cfg["dense_moe"]:
            return _moe_dense(h2p, ri, xres, m, *wts["moe"][layer], lng, lnb)
        return _moe(h2p, ri, xres, m, *wts["moe"][layer], lng, lnb, cfg["nbt"], cfg["tmx"], cfg["tc"])

    x2 = moe(h2p, ri, x1, m0, 0)

    k_sh, v_sh = _linear(x2, wts["kv_w"], ts, (F32, F32), tn=tn, name="kv_proj")

    m1 = mod[1]
    lam_init = 0.8 - 0.6 * math.exp(-0.3 * 1)
    (qq,) = _linear(x2, wts["b_w_q"], ts, (cfg["q_dtype"],), mod=m1, cols=(1, 0), out_scale=wts["q_scale"],
                    tn=tn, name="q_proj")
    if cache is None:
        o = _attn_prompt(qq.reshape(b, s, d), k_sh.reshape(b, s, d), v_sh.reshape(b, s, d),
                         wts["b_lambda"], wts["slopes"], lam_init, cfg["tq"], cfg["head_group"])
    else:
        cache_k, cache_v, page_table = cache
        hv = cache_k.shape[2:]
        o = _attn_decode(qq.reshape((b, s) + hv), k_sh.reshape((b, s) + hv), v_sh.reshape((b, s) + hv),
                         cache_k, cache_v, page_table, wts["b_lambda"], wts["slopes"], lam_init).reshape(b, s, d)
    heads_b = wts["slopes"].shape[0]
    x3, h2p, ri = _mixer_out(
        functools.partial(_mixb_body, heads=heads_b, gain=1.0 - lam_init), [rs(o)], x2, m1,
        [wts["b_subln_w"], wts["b_w_out"], wts["ln_g"][1][0], wts["ln_b"][1][0], *wts["router"][1]],
        cfg["ts_out"], cfg["packed"], "mix_b")
    y = moe(h2p, ri, x3, m1, 1)
    return (y.reshape(b, s, d), k_sh.reshape(b, s, d), v_sh.reshape(b, s, d), conv_new, c_new, n_new, m_new)


def kernel(x_prompt, x_sample, cache_k, cache_v, state_conv, state_C, state_n, state_m, page_table, c_prompt, c_sample, a_w_in, a_conv_w, a_conv_b, a_w_q, a_w_k, a_w_v, a_w_gate, a_b_gate, a_norm_w, a_skip, a_w_out, kv_w_k, kv_w_v, b_w_q, b_lambda, b_subln_w, b_w_out, moe_w_group, moe_b_group, moe_w_expert, moe_b_expert, moe_w1, moe_w3, moe_w2, ln_g, ln_b, ada_w, ada_b):
    bp, sp, d = x_prompt.shape
    db, ds, _ = x_sample.shape
    assert ds == 1
    heads_a = state_C.shape[2]
    heads_b = cache_k.shape[2]
    inner = a_conv_w.shape[2]
    hd_b = d // (2 * heads_b)
    ff = moe_w1.shape[-1]

    slopes = jnp.exp2(-8.0 * jnp.arange(1, heads_b + 1, dtype=F32) / heads_b)
    common = dict(
        heads_a=heads_a,
        a_norm_w=a_norm_w[0].reshape(1, inner), a_skip=a_skip[0].reshape(1, inner),
        q_scale=hd_b ** -0.5, b_lambda=b_lambda[0], b_subln_w=b_subln_w[0].reshape(1, -1), slopes=slopes,
        router=[_prep_router(moe_w_group[i], moe_b_group[i], moe_w_expert[i], moe_b_expert[i]) for i in range(DEPTH)],
        moe=[(moe_w1[i].reshape(N_EXPERTS, d, ff), moe_w3[i].reshape(N_EXPERTS, d, ff),
              moe_w2[i].reshape(N_EXPERTS, ff, d)) for i in range(DEPTH)],
        ln_g=[[ln_g[i, j].reshape(1, d) for j in range(2)] for i in range(DEPTH)],
        ln_b=[[ln_b[i, j].reshape(1, d) for j in range(2)] for i in range(DEPTH)],
    )

    def weights(dt):
        return dict(
            common,
            a_w_in=a_w_in[0].astype(dt),
            a_prep=_prep_mlstm(a_conv_w[0], a_conv_b[0], a_w_q[0], a_w_k[0], a_w_v[0], a_w_gate[0], a_b_gate[0],
                               heads_a, dt),
            a_w_out=a_w_out[0].astype(dt),
            kv_w=jnp.concatenate([kv_w_k, kv_w_v], axis=1).astype(dt),
            b_w_q=b_w_q[0].astype(dt), b_w_out=b_w_out[0].astype(dt))

    mod = _ada(jnp.concatenate([c_prompt, c_sample], axis=0), ada_w, ada_b)
    mod_p = mod[:, :bp].reshape(DEPTH, bp, 1, 6 * d)
    mod_s = mod[:, bp:].reshape(DEPTH, 1, db, 6 * d)

    cfg_p = dict(ts=512, ts_pre=256, ts_out=256, chunk=256, row_shape=(bp, sp),
                 nbt=sp * (2 if bp % 2 == 0 else 1), tmx=256, tc=512, tq=512,
                 head_group=1, tn=None, packed=True, dense_moe=False, q_dtype=BF16)
    zero_conv = jnp.zeros((bp, A_CONV - 1, inner), F32)
    outs_p = _trunk(x_prompt, mod_p, zero_conv, None, None, weights(BF16), cfg_p)

    cfg_s = dict(ts=db, ts_pre=1, ts_out=db, chunk=None, row_shape=(1, db), tn=d, packed=False, dense_moe=True,
                 q_dtype=F32)
    cache = (cache_k, cache_v, page_table)
    outs_s = _trunk(x_sample, mod_s, state_conv[0], (state_C[0], state_n[0], state_m[0]), cache, weights(F32), cfg_s)

    def pack(o, nb_, s_):
        y, k, v, conv, c, n, m = o
        return (y, k.reshape(nb_, s_, heads_b, 2 * hd_b), v.reshape(nb_, s_, heads_b, 2 * hd_b),
                conv[None], c[None], n[None], m[None])

    yp, kp, vp, convp, cp, np_, mp = pack(outs_p, bp, sp)
    ys, ks, vs, convs, cs, ns, ms_ = pack(outs_s, db, ds)
    return (yp, ys, kp, vp, convp, cp, np_, mp, ks, vs, convs, cs, ns, ms_)
```

```python
import functools
import math

import jax
import jax.numpy as jnp
from jax import lax
from jax.experimental import pallas as pl
from jax.experimental.pallas import tpu as pltpu

F32 = jnp.float32
BF16 = jnp.bfloat16
I32 = jnp.int32
U32 = jnp.uint32

DEPTH = 2
LN_EPS = 1e-5
DN_ALPHA = (2.0 * DEPTH) ** 0.25
A_CONV = 4
A_QKV_BLOCK = 4
MOE_GROUPS = 4
MOE_EXPERTS = 4
N_EXPERTS = MOE_GROUPS * MOE_EXPERTS
PAGE_SIZE = 128
LANES = 128
PAGES_PER_STEP = 8
MIB = 1024 * 1024


def _cparams(sem, vmem_mib=48):
    return pltpu.CompilerParams(dimension_semantics=sem, vmem_limit_bytes=vmem_mib * MIB)


def _silu(x):
    return x / (1.0 + jnp.exp(-x))


def _log_sigmoid(x):
    return jnp.minimum(x, 0.0) - jnp.log(1.0 + jnp.exp(-jnp.abs(x)))


def _dot(a, b):
    return jnp.dot(a, b, preferred_element_type=F32)


def _dot_nt(a, b):
    return lax.dot_general(a, b, (((1,), (1,)), ((), ())), preferred_element_type=F32)


def _dot_tn(a, b):
    return lax.dot_general(a, b, (((0,), (0,)), ((), ())), preferred_element_type=F32)


def _split(x):
    hi = x.astype(BF16)
    return hi, (x - hi.astype(F32)).astype(BF16)


def _dot3(a, w, dot=None):
    dot = dot or _dot
    ah, al = _split(a)
    wh, wl = _split(w)
    return dot(ah, wh) + dot(al, wh) + dot(ah, wl)


def _mm(a, w):
    if w.dtype == F32:
        return _dot3(a.astype(F32), w)
    return _dot(a.astype(BF16), w)


def _bits(x):
    return lax.bitcast_convert_type(x, U32)


def _pack_pairs(y):
    w = y.shape[-1] // 2
    r = _bits(y.astype(BF16).astype(F32))
    return (r[:, w:] & jnp.uint32(0xFFFF0000)) | (r[:, :w] >> 16)


def _unpack_pairs(u):
    lo = lax.bitcast_convert_type(u << 16, F32)
    hi = lax.bitcast_convert_type(u & jnp.uint32(0xFFFF0000), F32)
    return lo, hi


def _layer_norm(y, g, b):
    mu = jnp.mean(y, axis=-1, keepdims=True)
    yc = y - mu
    var = jnp.mean(yc * yc, axis=-1, keepdims=True)
    return yc * lax.rsqrt(var + LN_EPS) * g + b


def _ada_body(c_ref, w_ref, b_ref, o_ref):
    s = _silu(c_ref[...])
    o_ref[0] = _dot3(s, w_ref[0]) + b_ref[0]


def _ada(c_all, ada_w, ada_b):
    r, d = c_all.shape
    depth, _, n = ada_w.shape
    tn = 1536
    return pl.pallas_call(
        _ada_body,
        grid=(depth, n // tn),
        in_specs=[pl.BlockSpec((r, d), lambda i, j: (0, 0)),
                  pl.BlockSpec((1, d, tn), lambda i, j: (i, 0, j)),
                  pl.BlockSpec((1, 1, tn), lambda i, j: (i, 0, j))],
        out_specs=pl.BlockSpec((1, r, tn), lambda i, j: (i, 0, j)),
        out_shape=jax.ShapeDtypeStruct((depth, r, n), F32),
        compiler_params=_cparams(("parallel", "parallel")),
        name="ada",
    )(c_all, ada_w, ada_b.reshape(depth, 1, n))


def _mod_spec(mod, ts, col):
    d = mod.shape[-1] // 6
    if mod.shape[1] == 1:
        return pl.BlockSpec((1, 1, d), lambda b, s, *_: (b, 0, col))
    return pl.BlockSpec((1, ts, d), lambda b, s, *_: (b, s, col))


def _lin_body(*refs, has_mod, out_scale):
    if has_mod:
        x_ref, sc_ref, sh_ref, w_ref, *o_refs = refs
        x = x_ref[0] * (1.0 + sc_ref[0]) + sh_ref[0]
    else:
        x_ref, w_ref, *o_refs = refs
        x = x_ref[0]
    y = _mm(x, w_ref[...])
    if out_scale != 1.0:
        y = y * out_scale
    n = y.shape[-1] // len(o_refs)
    for j, o_ref in enumerate(o_refs):
        o_ref[0] = y[:, j * n:(j + 1) * n].astype(o_ref.dtype)


def _linear(x, w, ts, out_dtypes, mod=None, cols=None, out_scale=1.0, tn=None, name="linear"):
    b, s, k = x.shape
    n = w.shape[1]
    no = n // len(out_dtypes)
    tn = tn or n
    assert n % tn == 0 and (tn == n or no % tn == 0)
    in_specs = [pl.BlockSpec((1, ts, k), lambda i, j, c: (i, j, 0))]
    args = [x]
    if mod is not None:
        in_specs += [_mod_spec(mod, ts, cols[0]), _mod_spec(mod, ts, cols[1])]
        args += [mod, mod]
    in_specs.append(pl.BlockSpec((k, tn), lambda i, j, c: (0, c)))
    args.append(w)
    if tn == n:
        out_specs = [pl.BlockSpec((1, ts, no), lambda i, j, c: (i, j, 0)) for _ in out_dtypes]
        out_shape = [jax.ShapeDtypeStruct((b, s, no), dt) for dt in out_dtypes]
    else:
        assert len(set(out_dtypes)) == 1
        out_specs = [pl.BlockSpec((1, ts, tn), lambda i, j, c: (i, j, c))]
        out_shape = [jax.ShapeDtypeStruct((b, s, n), out_dtypes[0])]
    outs = pl.pallas_call(
        functools.partial(_lin_body, has_mod=mod is not None, out_scale=out_scale),
        grid=(b, s // ts, n // tn),
        in_specs=in_specs,
        out_specs=out_specs,
        out_shape=out_shape,
        compiler_params=_cparams(("parallel", "parallel", "parallel"), 56),
        name=name,
    )(*args)
    if tn == n:
        return outs
    return [outs[0][:, :, g * no:(g + 1) * no] for g in range(len(out_dtypes))]


def _p1_body(xm_ref, tail_ref, cw_ref, cb_ref, wqkv_ref, wg_ref, wgt_ref, bg_ref, bgt_ref,
             xc_ref, q_ref, k_ref, v_ref, g_ref, conv_ref, xbuf):
    s = pl.program_id(1)
    ts = xm_ref.shape[1]

    @pl.when(s == 0)
    def _():
        xbuf[0:8, :] = tail_ref[0]

    @pl.when(s > 0)
    def _():
        xbuf[0:8, :] = xbuf[ts:ts + 8, :]

    xm = xm_ref[0]
    xbuf[8:8 + ts, :] = xm
    conv = cb_ref[...] + xm * cw_ref[A_CONV - 1:A_CONV, :]
    for j in range(A_CONV - 1):
        conv = conv + xbuf[5 + j:5 + j + ts, :] * cw_ref[j:j + 1, :]
    conv_ref[0] = xbuf[ts:ts + 8, :]
    _p1_tail(_silu(conv), xm, wqkv_ref, wg_ref, wgt_ref, bg_ref, bgt_ref, xc_ref, q_ref, k_ref, v_ref, g_ref, True)


def _p1_rows_body(xm_ref, hist_ref, cw_ref, cb_ref, wqkv_ref, wg_ref, wgt_ref, bg_ref, bgt_ref,
                  xc_ref, q_ref, k_ref, v_ref, g_ref):
    xm = xm_ref[0]
    conv = cb_ref[...] + xm * cw_ref[A_CONV - 1:A_CONV, :]
    for j in range(A_CONV - 1):
        conv = conv + hist_ref[j] * cw_ref[j:j + 1, :]
    _p1_tail(_silu(conv), xm, wqkv_ref, wg_ref, wgt_ref, bg_ref, bgt_ref, xc_ref, q_ref, k_ref, v_ref, g_ref, False)


def _p1_tail(xc, xm, wqkv_ref, wg_ref, wgt_ref, bg_ref, bgt_ref, xc_ref, q_ref, k_ref, v_ref, g_ref, gates_on_lanes):
    inner = xm.shape[1]
    xc_ref[0] = xc
    cw = wqkv_ref.shape[2]
    for c in range(inner // cw):
        sl = slice(c * cw, (c + 1) * cw)
        q_ref[0, :, sl] = _mm(xc[:, sl], wqkv_ref[0, c]).astype(q_ref.dtype)
        k_ref[0, :, sl] = _mm(xc[:, sl], wqkv_ref[1, c]).astype(k_ref.dtype)
        v_ref[0, :, sl] = _mm(xm[:, sl], wqkv_ref[2, c]).astype(v_ref.dtype)
    q = q_ref[0]
    k = k_ref[0]
    v = v_ref[0]
    half = wgt_ref.shape[0] // 2
    if gates_on_lanes:
        gt = (_dot_nt(wgt_ref[:, 0:inner], q) + _dot_nt(wgt_ref[:, inner:2 * inner], k)
              + _dot_nt(wgt_ref[:, 2 * inner:3 * inner], v) + bgt_ref[...])
        sub = lax.broadcasted_iota(I32, gt.shape, 0)
        g_ref[0] = jnp.where(sub >= half, _log_sigmoid(gt), gt)
    else:
        g = (_mm(q, wg_ref[0:inner, :]) + _mm(k, wg_ref[inner:2 * inner, :])
             + _mm(v, wg_ref[2 * inner:3 * inner, :]) + bg_ref[...])
        lane = lax.broadcasted_iota(I32, g.shape, 1)
        g_ref[0] = jnp.where(lane >= half, _log_sigmoid(g), g)[:, 0:8]


def _mlstm_pre(xm, tail8, prep, ts):
    b, s, inner = xm.shape
    assert ts >= 8 and s % ts == 0
    full = lambda shape: pl.BlockSpec(shape, lambda i, j: (0,) * len(shape))
    row = lambda n, dt: (pl.BlockSpec((1, ts, n), lambda i, j: (i, j, 0)), jax.ShapeDtypeStruct((b, s, n), dt))
    qdt = BF16 if prep["wqkv"].dtype == BF16 else F32
    outs = [row(inner, F32), row(inner, qdt), row(inner, qdt), row(inner, qdt),
            (pl.BlockSpec((1, 8, ts), lambda i, j: (i, 0, j)), jax.ShapeDtypeStruct((b, 8, s), F32)),
            (pl.BlockSpec((1, 8, inner), lambda i, j: (i, 0, 0)), jax.ShapeDtypeStruct((b, 8, inner), F32))]
    return pl.pallas_call(
        _p1_body,
        grid=(b, s // ts),
        in_specs=[pl.BlockSpec((1, ts, inner), lambda i, j: (i, j, 0)),
                  pl.BlockSpec((1, 8, inner), lambda i, j: (i, 0, 0)),
                  full((A_CONV, inner)), full((1, inner)),
                  full(prep["wqkv"].shape),
                  full((3 * inner, LANES)), full((8, 3 * inner)), full((1, LANES)), full((8, 1))],
        out_specs=[o[0] for o in outs],
        out_shape=[o[1] for o in outs],
        scratch_shapes=[pltpu.VMEM((ts + 8, inner), F32)],
        compiler_params=_cparams(("parallel", "arbitrary"), 56),
        name="mlstm_pre",
    )(xm, tail8, prep["conv_w"], prep["conv_b"], prep["wqkv"],
      prep["wg"], prep["wgt"], prep["bg"], prep["bgt"])


def _mlstm_pre_rows(xm, hist, prep):
    _, r, inner = xm.shape
    full = lambda shape: pl.BlockSpec(shape, lambda i: (0,) * len(shape))
    row = lambda n, dt: (pl.BlockSpec((1, r, n), lambda i: (0, 0, 0)), jax.ShapeDtypeStruct((1, r, n), dt))
    qdt = BF16 if prep["wqkv"].dtype == BF16 else F32
    outs = [row(inner, F32), row(inner, qdt), row(inner, qdt), row(inner, qdt), row(8, F32)]
    return pl.pallas_call(
        _p1_rows_body,
        grid=(1,),
        in_specs=[full((1, r, inner)), full((A_CONV - 1, r, inner)),
                  full((A_CONV, inner)), full((1, inner)),
                  full(prep["wqkv"].shape),
                  full((3 * inner, LANES)), full((8, 3 * inner)), full((1, LANES)), full((8, 1))],
        out_specs=[o[0] for o in outs],
        out_shape=[o[1] for o in outs],
        compiler_params=_cparams(("arbitrary",), 56),
        name="mlstm_pre_rows",
    )(xm, hist, prep["conv_w"], prep["conv_b"], prep["wqkv"],
      prep["wg"], prep["wgt"], prep["bg"], prep["bgt"])


def _p2_body(q_ref, k_ref, v_ref, gr_ref, h_ref, c_ref, n_ref, m_ref, caug, ms):
    c = pl.program_id(1)
    nc = pl.num_programs(1)
    L = q_ref.shape[1]
    heads = gr_ref.shape[1]
    hd = q_ref.shape[2] // heads

    @pl.when(c == 0)
    def _():
        caug[...] = jnp.zeros_like(caug)
        ms[...] = jnp.zeros_like(ms)

    row = lax.broadcasted_iota(I32, (L, L), 0)
    col = lax.broadcasted_iota(I32, (L, L), 1)
    causal = col <= row
    diag = col == row
    lane = lax.broadcasted_iota(I32, (L, LANES), 1)
    ones_col = jnp.where(lane == 0, 1.0, 0.0).astype(BF16)
    for a in range(heads):
        sl = slice(a * hd, (a + 1) * hd)
        q = q_ref[0, :, sl]
        k = k_ref[0, :, sl]
        v = v_ref[0, :, sl]
        gr = gr_ref[0, a]
        ig_r, lf_r = gr[0:1, :], gr[1:2, :]
        ig_c = jnp.sum(jnp.where(diag, ig_r, 0.0), axis=1, keepdims=True)
        lf_c = jnp.sum(jnp.where(diag, lf_r, 0.0), axis=1, keepdims=True)
        b_c = jnp.sum(jnp.where(causal, lf_r, 0.0), axis=1, keepdims=True)
        b_r = jnp.sum(jnp.where(row <= col, lf_c, 0.0), axis=0, keepdims=True)
        log_d = jnp.where(causal, b_c - b_r + ig_r, -jnp.inf)
        m_prev = ms[a]
        log_inter = b_c + m_prev
        m_t = jnp.maximum(log_inter, jnp.max(log_d, axis=1, keepdims=True))
        d = jnp.exp(log_d - m_t)
        w_inter = jnp.exp(log_inter - m_t)
        s = (_dot_nt(q, k) * d).astype(BF16)
        vaug = jnp.concatenate([v, ones_col], axis=1)
        ca = caug[a]
        num = w_inter * _dot(q, ca.astype(BF16)) + _dot(s, vaug)
        den = num[:, hd:hd + 1]
        h_ref[0, :, sl] = num[:, 0:hd] / jnp.maximum(jnp.abs(den), jnp.exp(-m_t))
        m_new = m_t[L - 1:L, :]
        b_last = b_c[L - 1:L, :]
        w_s = jnp.exp(b_last - b_c + ig_c - m_new)
        decay = jnp.exp(b_last + m_prev - m_new)
        kw = (k.astype(F32) * w_s).astype(BF16)
        caug[a] = decay * ca + _dot_tn(kw, vaug)
        ms[a] = m_new

    @pl.when(c == nc - 1)
    def _():
        for a in range(heads):
            c_ref[0, a] = caug[a, :, 0:hd]
            n_ref[0, a] = caug[a, :, hd:hd + 1]
            m_ref[0, a] = ms[a]


def _mlstm_scan(q, k, v, grow, L):
    b, s, inner = q.shape
    heads = grow.shape[1]
    hd = inner // heads
    qkv = pl.BlockSpec((1, L, inner), lambda i, c: (i, c, 0))
    return pl.pallas_call(
        _p2_body,
        grid=(b, s // L),
        in_specs=[qkv, qkv, qkv,
                  pl.BlockSpec((1, heads, 2, L), lambda i, c: (i, 0, 0, c))],
        out_specs=[qkv,
                   pl.BlockSpec((1, heads, hd, hd), lambda i, c: (i, 0, 0, 0)),
                   pl.BlockSpec((1, heads, hd, 1), lambda i, c: (i, 0, 0, 0)),
                   pl.BlockSpec((1, heads, 1, 1), lambda i, c: (i, 0, 0, 0))],
        out_shape=[jax.ShapeDtypeStruct((b, s, inner), F32),
                   jax.ShapeDtypeStruct((b, heads, hd, hd), F32),
                   jax.ShapeDtypeStruct((b, heads, hd, 1), F32),
                   jax.ShapeDtypeStruct((b, heads, 1, 1), F32)],
        scratch_shapes=[pltpu.VMEM((heads, hd, hd + LANES), F32), pltpu.VMEM((heads, 1, 1), F32)],
        compiler_params=_cparams(("parallel", "arbitrary")),
        name="mlstm_scan",
    )(q, k, v, grow)


def _s2_body(q_ref, k_ref, v_ref, g_ref, c_ref, n_ref, m_ref, h_ref, co_ref, no_ref, mo_ref):
    heads = c_ref.shape[1]
    hd = c_ref.shape[2]
    sub = lax.broadcasted_iota(I32, (8, hd), 0)
    first = sub == 0
    for a in range(heads):
        sl = slice(a * hd, (a + 1) * hd)
        q = q_ref[0, :, sl]
        k = k_ref[0, :, sl]
        v = v_ref[0, :, sl]
        g = g_ref[0, a]
        ig, lf = g[:, 0:1], g[:, 1:2]
        m_prev = m_ref[0, a]
        m_new = jnp.maximum(lf + m_prev, ig)
        decay = jnp.exp(lf + m_prev - m_new)
        dd = jnp.exp(ig - m_new)
        k8 = jnp.where(first, k, 0.0)
        v8 = jnp.where(first, v, 0.0)
        q8 = jnp.where(first, q, 0.0)
        c_new = decay * c_ref[0, a] + dd * _dot3(k8, v8, _dot_tn)
        n_new = decay * n_ref[0, a] + dd * k
        num = _dot3(q8, c_new)[0:1, :]
        den = jnp.sum(q * n_new, axis=1, keepdims=True)
        h_ref[0, :, sl] = num / jnp.maximum(jnp.abs(den), jnp.exp(-m_new))
        co_ref[0, a] = c_new
        no_ref[0, a] = n_new
        mo_ref[0, a] = m_new


def _mlstm_step(q, k, v, g4, c0, n0, m0):
    b, _, inner = q.shape
    heads = c0.shape[1]
    hd = inner // heads
    qkv = pl.BlockSpec((1, 1, inner), lambda i: (i, 0, 0))
    st = lambda r, c: pl.BlockSpec((1, heads, r, c), lambda i: (i, 0, 0, 0))
    return pl.pallas_call(
        _s2_body,
        grid=(b,),
        in_specs=[qkv, qkv, qkv, st(1, 2), st(hd, hd), st(1, hd), st(1, 1)],
        out_specs=[qkv, st(hd, hd), st(1, hd), st(1, 1)],
        out_shape=[jax.ShapeDtypeStruct((b, 1, inner), F32),
                   jax.ShapeDtypeStruct((b, heads, hd, hd), F32),
                   jax.ShapeDtypeStruct((b, heads, 1, hd), F32),
                   jax.ShapeDtypeStruct((b, heads, 1, 1), F32)],
        compiler_params=_cparams(("parallel",)),
        name="mlstm_step",
    )(q, k, v, g4, c0, n0, m0)


def _route(logits):
    ts = logits.shape[0]
    lane = lax.broadcasted_iota(I32, (ts, LANES), 1)
    neg = -jnp.inf
    big = jnp.int32(1 << 20)
    is_g = lane < MOE_GROUPS
    gl = jnp.where(is_g, logits, neg)
    gmax = jnp.max(gl, axis=1, keepdims=True)
    gidx = jnp.min(jnp.where(gl == gmax, lane, big), axis=1, keepdims=True)
    gsum = jnp.sum(jnp.where(is_g, jnp.exp(logits - gmax), 0.0), axis=1, keepdims=True)
    g_w = 1.0 / gsum
    lo = MOE_GROUPS + MOE_EXPERTS * gidx
    el = jnp.where(lane >= lo, jnp.where(lane < lo + MOE_EXPERTS, logits, neg), neg)
    v1 = jnp.max(el, axis=1, keepdims=True)
    i1 = jnp.min(jnp.where(el == v1, lane, big), axis=1, keepdims=True)
    el2 = jnp.where(lane == i1, neg, el)
    v2 = jnp.max(el2, axis=1, keepdims=True)
    i2 = jnp.min(jnp.where(el2 == v2, lane, big), axis=1, keepdims=True)
    t = jnp.exp(v2 - v1)
    w1 = g_w / (1.0 + t)
    w2 = g_w * t / (1.0 + t)
    e1 = (i1 - MOE_GROUPS).astype(F32)
    e2 = (i2 - MOE_GROUPS).astype(F32)
    return jnp.where(lane == 0, e1, jnp.where(lane == 1, e2, jnp.where(lane == 2, w1, jnp.where(lane == 3, w2, 0.0))))


def _epilogue(x, o, g1, sc2, sh2, lng, lnb, wrh, wrl, br, x1_ref, h2_ref, ri_ref):
    x1 = _layer_norm(DN_ALPHA * x + (1.0 + g1) * o, lng, lnb)
    x1_ref[0] = x1
    h2 = x1 * (1.0 + sc2) + sh2
    h2_ref[0] = _pack_pairs(h2) if h2_ref.dtype == U32 else h2
    xh = h2.astype(BF16)
    xl = (h2 - xh.astype(F32)).astype(BF16)
    logits = _dot(xh, wrh) + _dot(xl, wrh) + _dot(xh, wrl) + br
    ri_ref[0] = _route(logits)


def _mixa_body(h_ref, xc_ref, z_ref, x_ref, g1_ref, sc2_ref, sh2_ref, nw_ref, skip_ref, wo_ref,
               lng_ref, lnb_ref, wrh_ref, wrl_ref, br_ref, x1_ref, h2_ref, ri_ref, *, heads):
    h = h_ref[0]
    inner = h.shape[1]
    hd = inner // heads
    parts = []
    for a in range(heads):
        ha = h[:, a * hd:(a + 1) * hd]
        hc = ha - jnp.mean(ha, axis=1, keepdims=True)
        parts.append(hc * lax.rsqrt(jnp.mean(hc * hc, axis=1, keepdims=True) + LN_EPS))
    hn = jnp.concatenate(parts, axis=1) * nw_ref[...]
    z = z_ref[0]
    out = (hn + skip_ref[...] * xc_ref[0]) * (1.0 / (1.0 + jnp.exp(-z)))
    o = _mm(out, wo_ref[...])
    _epilogue(x_ref[0], o, g1_ref[0], sc2_ref[0], sh2_ref[0], lng_ref[...], lnb_ref[...],
              wrh_ref[...], wrl_ref[...], br_ref[...], x1_ref, h2_ref, ri_ref)


def _mixb_body(o_ref, x_ref, g1_ref, sc2_ref, sh2_ref, sw_ref, wo_ref,
               lng_ref, lnb_ref, wrh_ref, wrl_ref, br_ref, x1_ref, h2_ref, ri_ref, *, heads, gain):
    o = o_ref[0]
    vd = o.shape[1] // heads
    parts = []
    for a in range(heads):
        oa = o[:, a * vd:(a + 1) * vd]
        parts.append(oa * lax.rsqrt(jnp.mean(oa * oa, axis=1, keepdims=True) + LN_EPS) * sw_ref[...] * gain)
    on = jnp.concatenate(parts, axis=1)
    y = _mm(on, wo_ref[...])
    _epilogue(x_ref[0], y, g1_ref[0], sc2_ref[0], sh2_ref[0], lng_ref[...], lnb_ref[...],
              wrh_ref[...], wrl_ref[...], br_ref[...], x1_ref, h2_ref, ri_ref)


def _mixer_out(body, acts, x, mod, consts, ts, packed, name):
    b, s, d = x.shape
    hw, hdt = (d // 2, U32) if packed else (d, F32)
    rowspec = lambda a: pl.BlockSpec((1, ts, a.shape[2]), lambda i, j: (i, j, 0))
    full = lambda a: pl.BlockSpec(a.shape, lambda i, j: (0,) * a.ndim)
    in_specs = [rowspec(a) for a in acts] + [rowspec(x)]
    in_specs += [_mod_spec(mod, ts, 2), _mod_spec(mod, ts, 4), _mod_spec(mod, ts, 3)]
    in_specs += [full(c) for c in consts]
    return pl.pallas_call(
        body,
        grid=(b, s // ts),
        in_specs=in_specs,
        out_specs=[pl.BlockSpec((1, ts, d), lambda i, j: (i, j, 0)),
                   pl.BlockSpec((1, ts, hw), lambda i, j: (i, j, 0)),
                   pl.BlockSpec((1, ts, LANES), lambda i, j: (i, j, 0))],
        out_shape=[jax.ShapeDtypeStruct((b, s, d), F32),
                   jax.ShapeDtypeStruct((b, s, hw), hdt),
                   jax.ShapeDtypeStruct((b, s, LANES), F32)],
        compiler_params=_cparams(("parallel", "parallel"), 56),
        name=name,
    )(*acts, x, mod, mod, mod, *consts)


def _sort_body(e1_ref, e2_ref, pa_ref, pb_ref, cnt_ref, *, tmx, cw):
    nb_tok = e1_ref.shape[2]
    e1 = e1_ref[0]
    e2 = e2_ref[0]
    sub = lax.broadcasted_iota(I32, (N_EXPERTS, nb_tok), 0)
    a_hot = sub == e1
    b_hot = sub == e2
    m = jnp.where(a_hot, 1.0, 0.0) + jnp.where(b_hot, 1.0, 0.0)
    cnt = jnp.sum(m, axis=1, keepdims=True)
    padded = jnp.floor((cnt + (tmx - 1)) / tmx) * tmx
    r16 = lax.broadcasted_iota(I32, (N_EXPERTS, N_EXPERTS), 0)
    c16 = lax.broadcasted_iota(I32, (N_EXPERTS, N_EXPERTS), 1)
    prow = jnp.sum(jnp.where(r16 == c16, padded, 0.0), axis=0, keepdims=True)
    segoff = jnp.sum(jnp.where(c16 < r16, prow, 0.0), axis=1, keepdims=True)
    cnt_ref[0] = jnp.broadcast_to(cnt, (N_EXPERTS, LANES)).astype(I32)
    ur = lax.broadcasted_iota(I32, (cw, cw), 0)
    uc = lax.broadcasted_iota(I32, (cw, cw), 1)
    upper = jnp.where(ur < uc, 1.0, 0.0).astype(BF16)
    carry = segoff
    for j in range(nb_tok // cw):
        sl = slice(j * cw, (j + 1) * cw)
        mc = m[:, sl]
        rank = _dot(mc.astype(BF16), upper) + carry
        pa_ref[0, :, sl] = jnp.sum(jnp.where(a_hot[:, sl], rank, 0.0), axis=0, keepdims=True).astype(I32)
        pb_ref[0, :, sl] = jnp.sum(jnp.where(b_hot[:, sl], rank, 0.0), axis=0, keepdims=True).astype(I32)
        carry = carry + jnp.sum(mc, axis=1, keepdims=True)


def _moe_sort(e1, e2, tmx):
    nb, _, nbt = e1.shape
    cw = min(nbt, 256)
    tok = pl.BlockSpec((1, 1, nbt), lambda i: (i, 0, 0))
    return pl.pallas_call(
        functools.partial(_sort_body, tmx=tmx, cw=cw),
        grid=(nb,),
        in_specs=[tok, tok],
        out_specs=[tok, tok, pl.BlockSpec((1, N_EXPERTS, LANES), lambda i: (i, 0, 0))],
        out_shape=[jax.ShapeDtypeStruct((nb, 1, nbt), I32), jax.ShapeDtypeStruct((nb, 1, nbt), I32),
                   jax.ShapeDtypeStruct((nb, N_EXPERTS, LANES), I32)],
        compiler_params=_cparams(("parallel",)),
        name="moe_sort",
    )(e1, e2)


def _dispatch_body(pa_ref, pb_ref, src_ref, dst_ref):
    nbt = src_ref.shape[1]
    dst_ref[...] = jnp.zeros_like(dst_ref)

    def body(i, carry):
        row = src_ref[0, pl.ds(i, 1), :]
        dst_ref[0, pl.ds(pa_ref[0, 0, i], 1), :] = row
        dst_ref[0, pl.ds(pb_ref[0, 0, i], 1), :] = row
        return carry

    lax.fori_loop(0, nbt, body, 0, unroll=min(8, nbt))


def _moe_dispatch(pa, pb, h2p, rp):
    nb, nbt, w = h2p.shape
    smem = pl.BlockSpec((1, 1, nbt), lambda i: (i, 0, 0), memory_space=pltpu.SMEM)
    return pl.pallas_call(
        _dispatch_body,
        grid=(nb,),
        in_specs=[smem, smem, pl.BlockSpec((1, nbt, w), lambda i: (i, 0, 0))],
        out_specs=pl.BlockSpec((1, rp, w), lambda i: (i, 0, 0), pipeline_mode=pl.Buffered(1)),
        out_shape=jax.ShapeDtypeStruct((nb, rp, w), h2p.dtype),
        compiler_params=_cparams(("arbitrary",), 56),
        name="moe_dispatch",
    )(pa, pb, h2p)


def _expert_body(tb_ref, tr_ref, te_ref, tv_ref, tf_ref, x_ref, w1_ref, w3_ref, w2_ref, o_ref, w1b, w3b, w2b):
    t = pl.program_id(0)

    @pl.when(tf_ref[t] == 1)
    def _():
        w1b[...] = w1_ref[0].astype(BF16)
        w3b[...] = w3_ref[0].astype(BF16)
        w2b[...] = w2_ref[0].astype(BF16)

    @pl.when(tv_ref[t] == 1)
    def _():
        if x_ref.dtype == U32:
            lo, hi = _unpack_pairs(x_ref[0])
            lo = lo.astype(BF16)
            hi = hi.astype(BF16)
            half = lo.shape[1]
            a = _dot(lo, w1b[0:half, :]) + _dot(hi, w1b[half:2 * half, :])
            g = _dot(lo, w3b[0:half, :]) + _dot(hi, w3b[half:2 * half, :])
        else:
            xb = x_ref[0].astype(BF16)
            a = _dot(xb, w1b[...])
            g = _dot(xb, w3b[...])
        y = _dot((_silu(a) * g).astype(BF16), w2b[...])
        o_ref[0] = _pack_pairs(y) if o_ref.dtype == U32 else y


def _moe_experts(tables, xs, w1, w3, w2, tmx):
    nb, rp, half = xs.shape
    ne, d, ff = w1.shape
    nt = tables[0].shape[0]
    xspec = pl.BlockSpec((1, tmx, half), lambda t, tb, tr, te, tv, tf: (tb[t], tr[t], 0))
    grid_spec = pltpu.PrefetchScalarGridSpec(
        num_scalar_prefetch=5,
        grid=(nt,),
        in_specs=[xspec,
                  pl.BlockSpec((1, d, ff), lambda t, tb, tr, te, tv, tf: (te[t], 0, 0)),
                  pl.BlockSpec((1, d, ff), lambda t, tb, tr, te, tv, tf: (te[t], 0, 0)),
                  pl.BlockSpec((1, ff, d), lambda t, tb, tr, te, tv, tf: (te[t], 0, 0))],
        out_specs=xspec,
        scratch_shapes=[pltpu.VMEM((d, ff), BF16), pltpu.VMEM((d, ff), BF16), pltpu.VMEM((ff, d), BF16)],
    )
    return pl.pallas_call(
        _expert_body,
        grid_spec=grid_spec,
        out_shape=jax.ShapeDtypeStruct((nb, rp, half), xs.dtype),
        compiler_params=_cparams(("arbitrary",)),
        name="moe_experts",
    )(*tables, xs, w1, w3, w2)


def _combine_body(pa_ref, pb_ref, ri_ref, ys_ref, x1_ref, g2_ref, lng_ref, lnb_ref, o_ref, ga, gb):
    tc = x1_ref.shape[1]

    def body(i, carry):
        ga[pl.ds(i, 1), :] = ys_ref[0, pl.ds(pa_ref[0, 0, i], 1), :]
        gb[pl.ds(i, 1), :] = ys_ref[0, pl.ds(pb_ref[0, 0, i], 1), :]
        return carry

    lax.fori_loop(0, tc, body, 0, unroll=min(8, tc))
    ri = ri_ref[0]
    wa = ri[:, 2:3]
    wb = ri[:, 3:4]
    if ga.dtype == U32:
        alo, ahi = _unpack_pairs(ga[...])
        blo, bhi = _unpack_pairs(gb[...])
        ffn = jnp.concatenate([wa * alo + wb * blo, wa * ahi + wb * bhi], axis=1)
    else:
        ffn = wa * ga[...] + wb * gb[...]
    y = DN_ALPHA * x1_ref[0] + (1.0 + g2_ref[0]) * ffn
    o_ref[0] = _layer_norm(y, lng_ref[...], lnb_ref[...])


def _moe_combine(pa, pb, ri, ys, x1, mod, lng, lnb, tc):
    nb, nbt, d = x1.shape
    rp, half = ys.shape[1:]
    smem = pl.BlockSpec((1, 1, tc), lambda i, j: (i, 0, j), memory_space=pltpu.SMEM)
    per_mod = (nb * nbt) // mod.shape[0]
    assert mod.shape[1] == 1 and per_mod % tc == 0
    return pl.pallas_call(
        _combine_body,
        grid=(nb, nbt // tc),
        in_specs=[smem, smem,
                  pl.BlockSpec((1, tc, LANES), lambda i, j: (i, j, 0)),
                  pl.BlockSpec((1, rp, half), lambda i, j: (i, 0, 0), pipeline_mode=pl.Buffered(1)),
                  pl.BlockSpec((1, tc, d), lambda i, j: (i, j, 0)),
                  pl.BlockSpec((1, 1, d), lambda i, j: ((i * nbt + j * tc) // per_mod, 0, 5)),
                  pl.BlockSpec((1, d), lambda i, j: (0, 0)),
                  pl.BlockSpec((1, d), lambda i, j: (0, 0))],
        out_specs=pl.BlockSpec((1, tc, d), lambda i, j: (i, j, 0)),
        out_shape=jax.ShapeDtypeStruct((nb, nbt, d), F32),
        scratch_shapes=[pltpu.VMEM((tc, half), ys.dtype), pltpu.VMEM((tc, half), ys.dtype)],
        compiler_params=_cparams(("parallel", "arbitrary"), 56),
        name="moe_combine",
    )(pa, pb, ri, ys, x1, mod, lng, lnb)


def _tile_tables(cnt, tmx, max_tiles):
    nb = cnt.shape[0]
    nt = (cnt + (tmx - 1)) // tmx
    first_tile = jnp.cumsum(nt, axis=1) - nt
    nt_eb = nt.T.reshape(-1)
    ends = jnp.cumsum(nt_eb)
    total = ends[-1]
    t = jnp.arange(max_tiles, dtype=I32)
    valid = t < total
    tc = jnp.minimum(t, total - 1)
    seg = jnp.sum((ends[None, :] <= tc[:, None]).astype(I32), axis=1)
    within = tc - (ends[seg] - nt_eb[seg])
    te = seg // nb
    tb = seg % nb
    tr = first_tile[tb, te] + within
    prev = jnp.concatenate([jnp.full((1,), -1, I32), te[:-1]])
    tf = (valid & (te != prev)).astype(I32)
    return tb.astype(I32), tr.astype(I32), te.astype(I32), valid.astype(I32), tf


def _moe(h2p, ri, x1, mod, w1, w3, w2, lng, lnb, nbt, tmx, tc):
    b, s, d = x1.shape
    nb = (b * s) // nbt
    rp = 2 * nbt + N_EXPERTS * tmx
    max_tiles = nb * (2 * nbt // tmx + N_EXPERTS)
    r = ri.reshape(nb, nbt, LANES)
    e1 = r[:, :, 0].astype(I32).reshape(nb, 1, nbt)
    e2 = r[:, :, 1].astype(I32).reshape(nb, 1, nbt)
    pa, pb, cnt = _moe_sort(e1, e2, tmx)
    tables = _tile_tables(cnt[:, :, 0], tmx, max_tiles)
    xs = _moe_dispatch(pa, pb, h2p.reshape(nb, nbt, h2p.shape[-1]), rp)
    ys = _moe_experts(tables, xs, w1, w3, w2, tmx)
    x2 = _moe_combine(pa, pb, r, ys, x1.reshape(nb, nbt, d), mod, lng, lnb, tc)
    return x2.reshape(b, s, d)


def _moe_dense_body(h_ref, ri_ref, x1_ref, g2_ref, lng_ref, lnb_ref, w1_ref, w3_ref, w2_ref, o_ref, acc):
    e = pl.program_id(0)

    @pl.when(e == 0)
    def _():
        acc[...] = jnp.zeros_like(acc)

    x = h_ref[0]
    ri = ri_ref[0]
    ef = e.astype(F32)
    comb = jnp.where(ri[:, 0:1] == ef, ri[:, 2:3], 0.0) + jnp.where(ri[:, 1:2] == ef, ri[:, 3:4], 0.0)
    hact = _silu(_dot3(x, w1_ref[0])) * _dot3(x, w3_ref[0]) * comb
    acc[...] += _dot3(hact, w2_ref[0])

    @pl.when(e == pl.num_programs(0) - 1)
    def _():
        y = DN_ALPHA * x1_ref[0] + (1.0 + g2_ref[0]) * acc[...]
        o_ref[0] = _layer_norm(y, lng_ref[...], lnb_ref[...])


def _moe_dense(h2, ri, x1, mod, w1, w3, w2, lng, lnb):
    _, r, d = x1.shape
    ne, _, ff = w1.shape
    row = lambda n: pl.BlockSpec((1, r, n), lambda e: (0, 0, 0))
    return pl.pallas_call(
        _moe_dense_body,
        grid=(ne,),
        in_specs=[row(d), row(LANES), row(d),
                  pl.BlockSpec((1, r, d), lambda e: (0, 0, 5)),
                  pl.BlockSpec((1, d), lambda e: (0, 0)), pl.BlockSpec((1, d), lambda e: (0, 0)),
                  pl.BlockSpec((1, d, ff), lambda e: (e, 0, 0)), pl.BlockSpec((1, d, ff), lambda e: (e, 0, 0)),
                  pl.BlockSpec((1, ff, d), lambda e: (e, 0, 0))],
        out_specs=row(d),
        out_shape=jax.ShapeDtypeStruct((1, r, d), F32),
        scratch_shapes=[pltpu.VMEM((r, d), F32)],
        compiler_params=_cparams(("arbitrary",)),
        name="moe_dense",
    )(h2, ri, x1, mod, lng, lnb, w1, w3, w2)


def _lambda(lam_ref):
    lp = lam_ref[...]
    a = jnp.sum(lp[0:1, :] * lp[1:2, :], axis=1, keepdims=True)
    c = jnp.sum(lp[2:3, :] * lp[3:4, :], axis=1, keepdims=True)
    return jnp.exp(a) - jnp.exp(c)


POS_SPLIT = 16


ONES_ROWS = 16


def _attn_body(slopes_ref, q_ref, k_ref, v_ref, lam_ref, o_ref, kaug, vt, qaug, m_s, acc, sbuf, *, lam_init, vd):
    hg = pl.program_id(1)
    j = pl.program_id(2)
    tq = q_ref.shape[1]
    group = q_ref.shape[2] // vd
    s_len = k_ref.shape[1]
    tk = tq

    @pl.when(j == 0)
    def _():
        pos = lax.broadcasted_iota(I32, (s_len, LANES), 0)
        ln = lax.broadcasted_iota(I32, (s_len, LANES), 1)
        hi = (pos // POS_SPLIT).astype(F32)
        lo = (pos % POS_SPLIT).astype(F32)
        ktail = jnp.where(ln == 0, hi, jnp.where(ln == 1, lo, 0.0)).astype(BF16)
        orow = lax.broadcasted_iota(I32, (ONES_ROWS, s_len), 0)
        ones_rows = jnp.where(orow == 0, 1.0, 0.0).astype(BF16)
        for g in range(group):
            kaug[g, :, 0:vd] = k_ref[0, :, g * vd:(g + 1) * vd].astype(BF16)
            kaug[g, :, vd:vd + LANES] = ktail
            vt[g, 0:vd, :] = v_ref[0, :, g * vd:(g + 1) * vd].T.astype(BF16)
            vt[g, vd:vd + ONES_ROWS, :] = ones_rows

    lane = lax.broadcasted_iota(I32, (tq, vd), 1)
    for g in range(group):
        slope = slopes_ref[hg * group + g]
        q = q_ref[0, :, g * vd:(g + 1) * vd].astype(F32)
        extra = jnp.where(lane == 0, slope * POS_SPLIT, jnp.where(lane == 1, slope, 0.0)).astype(BF16)
        qaug[g, 0:tq, 0:vd] = jnp.where(lane < vd // 2, q, 0.0).astype(BF16)
        qaug[g, 0:tq, vd:vd + LANES] = extra
        qaug[g, tq:2 * tq, 0:vd] = jnp.where(lane >= vd // 2, q, 0.0).astype(BF16)
        qaug[g, tq:2 * tq, vd:vd + LANES] = extra
    m_s[...] = jnp.full_like(m_s, -jnp.inf)
    acc[...] = jnp.zeros_like(acc)

    def scores(kb):
        off = pl.multiple_of(kb * tk, tk)
        return [_dot_nt(kaug[g, pl.ds(off, tk), :], qaug[g]) for g in range(group)]

    def consume(kb, masked):
        off = pl.multiple_of(kb * tk, tk)
        for g in range(group):
            st = sbuf[g]
            if masked:
                r = lax.broadcasted_iota(I32, st.shape, 0)
                c = lax.broadcasted_iota(I32, st.shape, 1)
                st = jnp.where(r <= (c & (tq - 1)), st, -jnp.inf)
            m_old = m_s[g]
            m_new = jnp.maximum(m_old, jnp.max(st, axis=0, keepdims=True))
            a = jnp.exp(m_old - m_new)
            p = jnp.exp(st - m_new)
            acc[g] = a * acc[g] + _dot(vt[g, :, pl.ds(off, tk)], p.astype(BF16))
            m_s[g] = m_new

    def put(sts):
        for g in range(group):
            sbuf[g] = sts[g]

    def off_diagonal(kb, carry):
        nxt = scores(kb + 1)
        consume(kb, False)
        put(nxt)
        return carry

    put(scores(0))
    lax.fori_loop(0, j, off_diagonal, 0)
    consume(j, True)
    lam = _lambda(lam_ref) + lam_init
    for g in range(group):
        num = acc[g, 0:vd, :]
        den = acc[g, vd:vd + 1, :]
        ot = num[:, 0:tq] / den[:, 0:tq] - lam * (num[:, tq:2 * tq] / den[:, tq:2 * tq])
        o_ref[0, :, g * vd:(g + 1) * vd] = ot.T


def _attn_prompt(q, k, v, lam_p, slopes, lam_init, tq, group):
    b, s, d = q.shape
    heads = slopes.shape[0]
    vd = d // heads
    gw = group * vd
    assert heads % group == 0 and tq & (tq - 1) == 0
    assert s // POS_SPLIT <= 256
    grid_spec = pltpu.PrefetchScalarGridSpec(
        num_scalar_prefetch=1,
        grid=(b, heads // group, s // tq),
        in_specs=[pl.BlockSpec((1, tq, gw), lambda i, h, j, sl: (i, j, h)),
                  pl.BlockSpec((1, s, gw), lambda i, h, j, sl: (i, 0, h)),
                  pl.BlockSpec((1, s, gw), lambda i, h, j, sl: (i, 0, h)),
                  pl.BlockSpec(lam_p.shape, lambda i, h, j, sl: (0, 0))],
        out_specs=pl.BlockSpec((1, tq, gw), lambda i, h, j, sl: (i, j, h)),
        scratch_shapes=[pltpu.VMEM((group, s, vd + LANES), BF16), pltpu.VMEM((group, vd + ONES_ROWS, s), BF16),
                        pltpu.VMEM((group, 2 * tq, vd + LANES), BF16),
                        pltpu.VMEM((group, 1, 2 * tq), F32),
                        pltpu.VMEM((group, vd + ONES_ROWS, 2 * tq), F32),
                        pltpu.VMEM((group, tq, 2 * tq), F32)],
    )
    return pl.pallas_call(
        functools.partial(_attn_body, lam_init=lam_init, vd=vd),
        grid_spec=grid_spec,
        out_shape=jax.ShapeDtypeStruct((b, s, d), F32),
        compiler_params=_cparams(("parallel", "parallel", "arbitrary"), 56),
        name="attn_prompt",
    )(slopes, q, k, v, lam_p)


def _decode_body(pt_ref, q_ref, kn_ref, vn_ref, slope_ref, lam_ref, *rest, lam_init, heads):
    npg = PAGES_PER_STEP
    k_refs = rest[:npg]
    v_refs = rest[npg:2 * npg]
    o_ref, m_s, l_s, acc, mb = rest[2 * npg:]
    j = pl.program_id(1)
    nj = pl.num_programs(1)
    vd = q_ref.shape[3]
    rows = 2 * heads
    keys = PAGE_SIZE * heads
    step_keys = npg * PAGE_SIZE
    lane = lax.broadcasted_iota(I32, (heads, vd), 1)
    qf = q_ref[0, 0].astype(F32)
    qm = jnp.concatenate([jnp.where(lane < vd // 2, qf, 0.0), jnp.where(lane >= vd // 2, qf, 0.0)], axis=0)
    shift = slope_ref[...] * float(step_keys)

    @pl.when(j == 0)
    def _():
        kn = kn_ref[0, 0]
        vn = vn_ref[0, 0]
        s_self = jnp.sum(qm * jnp.concatenate([kn, kn], axis=0), axis=1, keepdims=True)
        m_s[...] = s_self + shift * (1.0 * nj)
        l_s[...] = jnp.ones_like(l_s)
        acc[...] = jnp.concatenate([vn, vn], axis=0)
        r_i = lax.broadcasted_iota(I32, mb.shape, 0)
        c_i = lax.broadcasted_iota(I32, mb.shape, 1)
        dist = (step_keys - c_i // heads).astype(F32)
        mb[...] = jnp.where((c_i % heads) == (r_i % heads), -slope_ref[...] * dist, -jnp.inf)

    qh, ql = _split(qm)
    qhl = jnp.concatenate([qh, ql], axis=0)

    def scores(i):
        both = _dot_nt(qhl, k_refs[i][0].reshape(keys, vd).astype(BF16))
        return both[0:rows] + both[rows:2 * rows]

    s = jnp.concatenate([scores(i) for i in range(npg)], axis=1) + mb[...]
    m_old = m_s[...] - shift
    m_new = jnp.maximum(m_old, jnp.max(s, axis=1, keepdims=True))
    a = jnp.exp(m_old - m_new)
    p = jnp.exp(s - m_new)
    l_s[...] = a * l_s[...] + jnp.sum(p, axis=1, keepdims=True)
    ph, plo = _split(p)
    phl = jnp.concatenate([ph, plo], axis=0)
    pv = jnp.zeros((rows, vd), F32)
    for i in range(npg):
        both = _dot(phl[:, i * keys:(i + 1) * keys], v_refs[i][0].reshape(keys, vd).astype(BF16))
        pv = pv + both[0:rows] + both[rows:2 * rows]
    acc[...] = a * acc[...] + pv
    m_s[...] = m_new

    @pl.when(j == nj - 1)
    def _():
        lam = _lambda(lam_ref) + lam_init
        o16 = acc[...] / l_s[...]
        o_ref[0, 0] = o16[0:heads, :] - lam * o16[heads:rows, :]


def _attn_decode(q, k_new, v_new, cache_k, cache_v, page_table, lam_p, slopes, lam_init):
    db, _, heads, vd = q.shape
    n_pages = page_table.shape[1]
    npg = PAGES_PER_STEP
    assert n_pages % npg == 0
    rows = 2 * heads
    slope16 = jnp.concatenate([slopes, slopes]).reshape(rows, 1)
    tok = pl.BlockSpec((1, 1, heads, vd), lambda i, j, pt: (i, 0, 0, 0))

    def page_spec(idx):
        return pl.BlockSpec((1, PAGE_SIZE, heads, vd), lambda i, j, pt: (pt[i, j * npg + idx], 0, 0, 0))

    grid_spec = pltpu.PrefetchScalarGridSpec(
        num_scalar_prefetch=1,
        grid=(db, n_pages // npg),
        in_specs=[tok, tok, tok,
                  pl.BlockSpec((rows, 1), lambda i, j, pt: (0, 0)),
                  pl.BlockSpec(lam_p.shape, lambda i, j, pt: (0, 0))]
                 + [page_spec(i) for i in range(npg)] + [page_spec(i) for i in range(npg)],
        out_specs=tok,
        scratch_shapes=[pltpu.VMEM((rows, 1), F32), pltpu.VMEM((rows, 1), F32), pltpu.VMEM((rows, vd), F32),
                        pltpu.VMEM((rows, npg * PAGE_SIZE * heads), F32)],
    )
    return pl.pallas_call(
        functools.partial(_decode_body, lam_init=lam_init, heads=heads),
        grid_spec=grid_spec,
        out_shape=jax.ShapeDtypeStruct((db, 1, heads, vd), F32),
        compiler_params=_cparams(("parallel", "arbitrary")),
        name="attn_decode",
    )(page_table, q, k_new, v_new, slope16, lam_p, *([cache_k] * npg), *([cache_v] * npg))


MXU_WIDTH = 256


def _headwise_blocks(w):
    rows = w.reshape(-1, MXU_WIDTH, A_QKV_BLOCK)
    tiled = jnp.tile(rows, (1, 1, MXU_WIDTH // A_QKV_BLOCK))
    r = lax.broadcasted_iota(I32, tiled.shape, 1) // A_QKV_BLOCK
    s = lax.broadcasted_iota(I32, tiled.shape, 2) // A_QKV_BLOCK
    return jnp.where(r == s, tiled, 0.0)


def _prep_mlstm(conv_w, conv_b, w_q, w_k, w_v, w_gate, b_gate, heads, dt):
    inner = conv_w.shape[1]
    hd = inner // heads
    ng = b_gate.shape[0]
    wg = w_gate.reshape(heads, 3, hd, ng).transpose(1, 0, 2, 3).reshape(3 * inner, ng)
    wgp = jnp.pad(wg, ((0, 0), (0, LANES - ng)))
    return dict(conv_w=conv_w, conv_b=conv_b.reshape(1, inner),
                wqkv=jnp.stack([_headwise_blocks(w_q), _headwise_blocks(w_k) * hd ** -0.5,
                                _headwise_blocks(w_v)]).astype(dt),
                wg=wgp.astype(dt), wgt=wg.T.astype(BF16),
                bg=jnp.pad(b_gate, (0, LANES - ng)).reshape(1, LANES), bgt=b_gate.reshape(ng, 1))


def _prep_router(w_group, b_group, w_expert, b_expert):
    d = w_group.shape[0]
    we = w_expert.transpose(1, 0, 2).reshape(d, N_EXPERTS)
    wr = jnp.pad(jnp.concatenate([w_group, we], axis=1), ((0, 0), (0, LANES - MOE_GROUPS - N_EXPERTS)))
    br = jnp.pad(jnp.concatenate([b_group, b_expert.reshape(-1)]), (0, LANES - MOE_GROUPS - N_EXPERTS))
    wrh = wr.astype(BF16)
    wrl = (wr - wrh.astype(F32)).astype(BF16)
    return wrh, wrl, br.reshape(1, LANES)


def _trunk(x, mod, conv0, state, cache, wts, cfg):
    b, s, d = x.shape
    heads_a = wts["heads_a"]
    ts = cfg["ts"]
    row_shape = cfg["row_shape"]
    xr = x.reshape(row_shape + (d,))

    m0 = mod[0]
    tn = cfg["tn"]
    xm, z = _linear(xr, wts["a_w_in"], ts, (F32, F32), mod=m0, cols=(1, 0), tn=tn, name="a_in")
    inner = xm.shape[-1]
    hd = inner // heads_a
    if cfg["chunk"] is not None:
        tail8 = jnp.pad(conv0, ((0, 0), (8 - (A_CONV - 1), 0), (0, 0)))
        xc, q, k, v, gt, conv8 = _mlstm_pre(xm, tail8, wts["a_prep"], cfg["ts_pre"])
        conv_new = conv8[:, 8 - (A_CONV - 1):, :]
        grow = gt.reshape(b, 2, heads_a, s).transpose(0, 2, 1, 3)
        hh, c_new, n_new, m_new = _mlstm_scan(q, k, v, grow, cfg["chunk"])
    else:
        xc, q, k, v, g = _mlstm_pre_rows(xm, conv0.transpose(1, 0, 2), wts["a_prep"])
        conv_new = jnp.concatenate([conv0[:, 1:, :], xm.reshape(b, s, inner)], axis=1)
        c0, n0, mm0 = state
        g4 = g.reshape(b, 2, heads_a).transpose(0, 2, 1).reshape(b, heads_a, 1, 2)
        tok = lambda a: a.reshape(b, s, inner)
        hh, c_new, n_new, m_new = _mlstm_step(tok(q), tok(k), tok(v), g4, c0, n0.reshape(b, heads_a, 1, hd),
                                              mm0.reshape(b, heads_a, 1, 1))
    n_new = n_new.reshape(b, heads_a, hd)
    m_new = m_new.reshape(b, heads_a)
    rs = lambda a: a.reshape(row_shape + (a.shape[-1],))
    x1, h2p, ri = _mixer_out(
        functools.partial(_mixa_body, heads=heads_a), [rs(hh), rs(xc), z], xr, m0,
        [wts["a_norm_w"], wts["a_skip"], wts["a_w_out"], wts["ln_g"][0][0], wts["ln_b"][0][0], *wts["router"][0]],
        cfg["ts_out"], cfg["packed"], "mix_a")

    def moe(h2p, ri, xres, m, layer):
        lng, lnb = wts["ln_g"][layer][1], wts["ln_b"][layer][1]
        if cfg["dense_moe"]:
            return _moe_dense(h2p, ri, xres, m, *wts["moe"][layer], lng, lnb)
        return _moe(h2p, ri, xres, m, *wts["moe"][layer], lng, lnb, cfg["nbt"], cfg["tmx"], cfg["tc"])

    x2 = moe(h2p, ri, x1, m0, 0)

    k_sh, v_sh = _linear(x2, wts["kv_w"], ts, (F32, F32), tn=tn, name="kv_proj")

    m1 = mod[1]
    lam_init = 0.8 - 0.6 * math.exp(-0.3 * 1)
    (qq,) = _linear(x2, wts["b_w_q"], ts, (cfg["q_dtype"],), mod=m1, cols=(1, 0), out_scale=wts["q_scale"],
                    tn=tn, name="q_proj")
    if cache is None:
        o = _attn_prompt(qq.reshape(b, s, d), k_sh.reshape(b, s, d), v_sh.reshape(b, s, d),
                         wts["b_lambda"], wts["slopes"], lam_init, cfg["tq"], cfg["head_group"])
    else:
        cache_k, cache_v, page_table = cache
        hv = cache_k.shape[2:]
        o = _attn_decode(qq.reshape((b, s) + hv), k_sh.reshape((b, s) + hv), v_sh.reshape((b, s) + hv),
                         cache_k, cache_v, page_table, wts["b_lambda"], wts["slopes"], lam_init).reshape(b, s, d)
    heads_b = wts["slopes"].shape[0]
    x3, h2p, ri = _mixer_out(
        functools.partial(_mixb_body, heads=heads_b, gain=1.0 - lam_init), [rs(o)], x2, m1,
        [wts["b_subln_w"], wts["b_w_out"], wts["ln_g"][1][0], wts["ln_b"][1][0], *wts["router"][1]],
        cfg["ts_out"], cfg["packed"], "mix_b")
    y = moe(h2p, ri, x3, m1, 1)
    return (y.reshape(b, s, d), k_sh.reshape(b, s, d), v_sh.reshape(b, s, d), conv_new, c_new, n_new, m_new)


def kernel(x_prompt, x_sample, cache_k, cache_v, state_conv, state_C, state_n, state_m, page_table, c_prompt, c_sample, a_w_in, a_conv_w, a_conv_b, a_w_q, a_w_k, a_w_v, a_w_gate, a_b_gate, a_norm_w, a_skip, a_w_out, kv_w_k, kv_w_v, b_w_q, b_lambda, b_subln_w, b_w_out, moe_w_group, moe_b_group, moe_w_expert, moe_b_expert, moe_w1, moe_w3, moe_w2, ln_g, ln_b, ada_w, ada_b):
    bp, sp, d = x_prompt.shape
    db, ds, _ = x_sample.shape
    assert ds == 1
    heads_a = state_C.shape[2]
    heads_b = cache_k.shape[2]
    inner = a_conv_w.shape[2]
    hd_b = d // (2 * heads_b)
    ff = moe_w1.shape[-1]

    slopes = jnp.exp2(-8.0 * jnp.arange(1, heads_b + 1, dtype=F32) / heads_b)
    common = dict(
        heads_a=heads_a,
        a_norm_w=a_norm_w[0].reshape(1, inner), a_skip=a_skip[0].reshape(1, inner),
        q_scale=hd_b ** -0.5, b_lambda=b_lambda[0], b_subln_w=b_subln_w[0].reshape(1, -1), slopes=slopes,
        router=[_prep_router(moe_w_group[i], moe_b_group[i], moe_w_expert[i], moe_b_expert[i]) for i in range(DEPTH)],
        moe=[(moe_w1[i].reshape(N_EXPERTS, d, ff), moe_w3[i].reshape(N_EXPERTS, d, ff),
              moe_w2[i].reshape(N_EXPERTS, ff, d)) for i in range(DEPTH)],
        ln_g=[[ln_g[i, j].reshape(1, d) for j in range(2)] for i in range(DEPTH)],
        ln_b=[[ln_b[i, j].reshape(1, d) for j in range(2)] for i in range(DEPTH)],
    )

    def weights(dt):
        return dict(
            common,
            a_w_in=a_w_in[0].astype(dt),
            a_prep=_prep_mlstm(a_conv_w[0], a_conv_b[0], a_w_q[0], a_w_k[0], a_w_v[0], a_w_gate[0], a_b_gate[0],
                               heads_a, dt),
            a_w_out=a_w_out[0].astype(dt),
            kv_w=jnp.concatenate([kv_w_k, kv_w_v], axis=1).astype(dt),
            b_w_q=b_w_q[0].astype(dt), b_w_out=b_w_out[0].astype(dt))

    mod = _ada(jnp.concatenate([c_prompt, c_sample], axis=0), ada_w, ada_b)
    mod_p = mod[:, :bp].reshape(DEPTH, bp, 1, 6 * d)
    mod_s = mod[:, bp:].reshape(DEPTH, 1, db, 6 * d)

    cfg_p = dict(ts=512, ts_pre=256, ts_out=256, chunk=256, row_shape=(bp, sp),
                 nbt=sp, tmx=256, tc=512, tq=512,
                 head_group=4, tn=None, packed=False, dense_moe=False, q_dtype=BF16)
    zero_conv = jnp.zeros((bp, A_CONV - 1, inner), F32)
    outs_p = _trunk(x_prompt, mod_p, zero_conv, None, None, weights(BF16), cfg_p)

    cfg_s = dict(ts=db, ts_pre=1, ts_out=db, chunk=None, row_shape=(1, db), tn=d, packed=False, dense_moe=True,
                 q_dtype=F32)
    cache = (cache_k, cache_v, page_table)
    outs_s = _trunk(x_sample, mod_s, state_conv[0], (state_C[0], state_n[0], state_m[0]), cache, weights(F32), cfg_s)

    def pack(o, nb_, s_):
        y, k, v, conv, c, n, m = o
        return (y, k.reshape(nb_, s_, heads_b, 2 * hd_b), v.reshape(nb_, s_, heads_b, 2 * hd_b),
                conv[None], c[None], n[None], m[None])

    yp, kp, vp, convp, cp, np_, mp = pack(outs_p, bp, sp)
    ys, ks, vs, convs, cs, ns, ms_ = pack(outs_s, db, ds)
    return (yp, ys, kp, vp, convp, cp, np_, mp, ks, vs, convs, cs, ns, ms_)
```

```python
import functools
import math

import jax
import jax.numpy as jnp
from jax import lax
from jax.experimental import pallas as pl
from jax.experimental.pallas import tpu as pltpu

F32 = jnp.float32
BF16 = jnp.bfloat16
I32 = jnp.int32
U32 = jnp.uint32

DEPTH = 2
LN_EPS = 1e-5
DN_ALPHA = (2.0 * DEPTH) ** 0.25
A_CONV = 4
A_QKV_BLOCK = 4
MOE_GROUPS = 4
MOE_EXPERTS = 4
N_EXPERTS = MOE_GROUPS * MOE_EXPERTS
PAGE_SIZE = 128
LANES = 128
PAGES_PER_STEP = 16
MIB = 1024 * 1024


def _cparams(sem, vmem_mib=48):
    return pltpu.CompilerParams(dimension_semantics=sem, vmem_limit_bytes=vmem_mib * MIB)


def _silu(x):
    return x / (1.0 + jnp.exp(-x))


def _log_sigmoid(x):
    return jnp.minimum(x, 0.0) - jnp.log(1.0 + jnp.exp(-jnp.abs(x)))


def _dot(a, b):
    return jnp.dot(a, b, preferred_element_type=F32)


def _dot_nt(a, b):
    return lax.dot_general(a, b, (((1,), (1,)), ((), ())), preferred_element_type=F32)


def _dot_tn(a, b):
    return lax.dot_general(a, b, (((0,), (0,)), ((), ())), preferred_element_type=F32)


def _split(x):
    hi = x.astype(BF16)
    return hi, (x - hi.astype(F32)).astype(BF16)


def _dot3(a, w, dot=None):
    dot = dot or _dot
    ah, al = _split(a)
    wh, wl = _split(w)
    return dot(ah, wh) + dot(al, wh) + dot(ah, wl)


def _mm(a, w):
    if w.dtype == F32:
        return _dot3(a.astype(F32), w)
    return _dot(a.astype(BF16), w)


def _bits(x):
    return lax.bitcast_convert_type(x, U32)


def _pack_pairs(y):
    w = y.shape[-1] // 2
    r = _bits(y.astype(BF16).astype(F32))
    return (r[:, w:] & jnp.uint32(0xFFFF0000)) | (r[:, :w] >> 16)


def _unpack_pairs(u):
    lo = lax.bitcast_convert_type(u << 16, F32)
    hi = lax.bitcast_convert_type(u & jnp.uint32(0xFFFF0000), F32)
    return lo, hi


def _layer_norm(y, g, b):
    mu = jnp.mean(y, axis=-1, keepdims=True)
    yc = y - mu
    var = jnp.mean(yc * yc, axis=-1, keepdims=True)
    return yc * lax.rsqrt(var + LN_EPS) * g + b


def _ada_body(c_ref, w_ref, b_ref, o_ref):
    s = _silu(c_ref[...])
    o_ref[0] = _dot3(s, w_ref[0]) + b_ref[0]


def _ada(c_all, ada_w, ada_b):
    r, d = c_all.shape
    depth, _, n = ada_w.shape
    tn = 1536
    return pl.pallas_call(
        _ada_body,
        grid=(depth, n // tn),
        in_specs=[pl.BlockSpec((r, d), lambda i, j: (0, 0)),
                  pl.BlockSpec((1, d, tn), lambda i, j: (i, 0, j)),
                  pl.BlockSpec((1, 1, tn), lambda i, j: (i, 0, j))],
        out_specs=pl.BlockSpec((1, r, tn), lambda i, j: (i, 0, j)),
        out_shape=jax.ShapeDtypeStruct((depth, r, n), F32),
        compiler_params=_cparams(("parallel", "parallel")),
        name="ada",
    )(c_all, ada_w, ada_b.reshape(depth, 1, n))


def _mod_spec(mod, ts, col):
    d = mod.shape[-1] // 6
    if mod.shape[1] == 1:
        return pl.BlockSpec((1, 1, d), lambda b, s, *_: (b, 0, col))
    return pl.BlockSpec((1, ts, d), lambda b, s, *_: (b, s, col))


def _lin_body(*refs, has_mod, out_scale):
    if has_mod:
        x_ref, sc_ref, sh_ref, w_ref, *o_refs = refs
        x = x_ref[0] * (1.0 + sc_ref[0]) + sh_ref[0]
    else:
        x_ref, w_ref, *o_refs = refs
        x = x_ref[0]
    y = _mm(x, w_ref[...])
    if out_scale != 1.0:
        y = y * out_scale
    n = y.shape[-1] // len(o_refs)
    for j, o_ref in enumerate(o_refs):
        o_ref[0] = y[:, j * n:(j + 1) * n].astype(o_ref.dtype)


def _linear(x, w, ts, out_dtypes, mod=None, cols=None, out_scale=1.0, tn=None, name="linear"):
    b, s, k = x.shape
    n = w.shape[1]
    no = n // len(out_dtypes)
    tn = tn or n
    assert n % tn == 0 and (tn == n or no % tn == 0)
    in_specs = [pl.BlockSpec((1, ts, k), lambda i, j, c: (i, j, 0))]
    args = [x]
    if mod is not None:
        in_specs += [_mod_spec(mod, ts, cols[0]), _mod_spec(mod, ts, cols[1])]
        args += [mod, mod]
    in_specs.append(pl.BlockSpec((k, tn), lambda i, j, c: (0, c)))
    args.append(w)
    if tn == n:
        out_specs = [pl.BlockSpec((1, ts, no), lambda i, j, c: (i, j, 0)) for _ in out_dtypes]
        out_shape = [jax.ShapeDtypeStruct((b, s, no), dt) for dt in out_dtypes]
    else:
        assert len(set(out_dtypes)) == 1
        out_specs = [pl.BlockSpec((1, ts, tn), lambda i, j, c: (i, j, c))]
        out_shape = [jax.ShapeDtypeStruct((b, s, n), out_dtypes[0])]
    outs = pl.pallas_call(
        functools.partial(_lin_body, has_mod=mod is not None, out_scale=out_scale),
        grid=(b, s // ts, n // tn),
        in_specs=in_specs,
        out_specs=out_specs,
        out_shape=out_shape,
        compiler_params=_cparams(("parallel", "parallel", "parallel"), 56),
        name=name,
    )(*args)
    if tn == n:
        return outs
    return [outs[0][:, :, g * no:(g + 1) * no] for g in range(len(out_dtypes))]


def _p1_body(xm_ref, tail_ref, cw_ref, cb_ref, wqkv_ref, wg_ref, wgt_ref, bg_ref, bgt_ref,
             xc_ref, q_ref, k_ref, v_ref, g_ref, conv_ref, xbuf):
    s = pl.program_id(1)
    ts = xm_ref.shape[1]

    @pl.when(s == 0)
    def _():
        xbuf[0:8, :] = tail_ref[0]

    @pl.when(s > 0)
    def _():
        xbuf[0:8, :] = xbuf[ts:ts + 8, :]

    xm = xm_ref[0]
    xbuf[8:8 + ts, :] = xm
    conv = cb_ref[...] + xm * cw_ref[A_CONV - 1:A_CONV, :]
    for j in range(A_CONV - 1):
        conv = conv + xbuf[5 + j:5 + j + ts, :] * cw_ref[j:j + 1, :]
    conv_ref[0] = xbuf[ts:ts + 8, :]
    _p1_tail(_silu(conv), xm, wqkv_ref, wg_ref, wgt_ref, bg_ref, bgt_ref, xc_ref, q_ref, k_ref, v_ref, g_ref, True)


def _p1_rows_body(xm_ref, hist_ref, cw_ref, cb_ref, wqkv_ref, wg_ref, wgt_ref, bg_ref, bgt_ref,
                  xc_ref, q_ref, k_ref, v_ref, g_ref):
    xm = xm_ref[0]
    conv = cb_ref[...] + xm * cw_ref[A_CONV - 1:A_CONV, :]
    for j in range(A_CONV - 1):
        conv = conv + hist_ref[j] * cw_ref[j:j + 1, :]
    _p1_tail(_silu(conv), xm, wqkv_ref, wg_ref, wgt_ref, bg_ref, bgt_ref, xc_ref, q_ref, k_ref, v_ref, g_ref, False)


def _p1_tail(xc, xm, wqkv_ref, wg_ref, wgt_ref, bg_ref, bgt_ref, xc_ref, q_ref, k_ref, v_ref, g_ref, gates_on_lanes):
    inner = xm.shape[1]
    xc_ref[0] = xc
    cw = wqkv_ref.shape[2]
    for c in range(inner // cw):
        sl = slice(c * cw, (c + 1) * cw)
        q_ref[0, :, sl] = _mm(xc[:, sl], wqkv_ref[0, c]).astype(q_ref.dtype)
        k_ref[0, :, sl] = _mm(xc[:, sl], wqkv_ref[1, c]).astype(k_ref.dtype)
        v_ref[0, :, sl] = _mm(xm[:, sl], wqkv_ref[2, c]).astype(v_ref.dtype)
    q = q_ref[0]
    k = k_ref[0]
    v = v_ref[0]
    half = wgt_ref.shape[0] // 2
    if gates_on_lanes:
        gt = (_dot_nt(wgt_ref[:, 0:inner], q) + _dot_nt(wgt_ref[:, inner:2 * inner], k)
              + _dot_nt(wgt_ref[:, 2 * inner:3 * inner], v) + bgt_ref[...])
        sub = lax.broadcasted_iota(I32, gt.shape, 0)
        g_ref[0] = jnp.where(sub >= half, _log_sigmoid(gt), gt)
    else:
        g = (_mm(q, wg_ref[0:inner, :]) + _mm(k, wg_ref[inner:2 * inner, :])
             + _mm(v, wg_ref[2 * inner:3 * inner, :]) + bg_ref[...])
        lane = lax.broadcasted_iota(I32, g.shape, 1)
        g_ref[0] = jnp.where(lane >= half, _log_sigmoid(g), g)[:, 0:8]


def _mlstm_pre(xm, tail8, prep, ts):
    b, s, inner = xm.shape
    assert ts >= 8 and s % ts == 0
    full = lambda shape: pl.BlockSpec(shape, lambda i, j: (0,) * len(shape))
    row = lambda n, dt: (pl.BlockSpec((1, ts, n), lambda i, j: (i, j, 0)), jax.ShapeDtypeStruct((b, s, n), dt))
    qdt = BF16 if prep["wqkv"].dtype == BF16 else F32
    outs = [row(inner, F32), row(inner, qdt), row(inner, qdt), row(inner, qdt),
            (pl.BlockSpec((1, 8, ts), lambda i, j: (i, 0, j)), jax.ShapeDtypeStruct((b, 8, s), F32)),
            (pl.BlockSpec((1, 8, inner), lambda i, j: (i, 0, 0)), jax.ShapeDtypeStruct((b, 8, inner), F32))]
    return pl.pallas_call(
        _p1_body,
        grid=(b, s // ts),
        in_specs=[pl.BlockSpec((1, ts, inner), lambda i, j: (i, j, 0)),
                  pl.BlockSpec((1, 8, inner), lambda i, j: (i, 0, 0)),
                  full((A_CONV, inner)), full((1, inner)),
                  full(prep["wqkv"].shape),
                  full((3 * inner, LANES)), full((8, 3 * inner)), full((1, LANES)), full((8, 1))],
        out_specs=[o[0] for o in outs],
        out_shape=[o[1] for o in outs],
        scratch_shapes=[pltpu.VMEM((ts + 8, inner), F32)],
        compiler_params=_cparams(("parallel", "arbitrary"), 56),
        name="mlstm_pre",
    )(xm, tail8, prep["conv_w"], prep["conv_b"], prep["wqkv"],
      prep["wg"], prep["wgt"], prep["bg"], prep["bgt"])


def _mlstm_pre_rows(xm, hist, prep):
    _, r, inner = xm.shape
    full = lambda shape: pl.BlockSpec(shape, lambda i: (0,) * len(shape))
    row = lambda n, dt: (pl.BlockSpec((1, r, n), lambda i: (0, 0, 0)), jax.ShapeDtypeStruct((1, r, n), dt))
    qdt = BF16 if prep["wqkv"].dtype == BF16 else F32
    outs = [row(inner, F32), row(inner, qdt), row(inner, qdt), row(inner, qdt), row(8, F32)]
    return pl.pallas_call(
        _p1_rows_body,
        grid=(1,),
        in_specs=[full((1, r, inner)), full((A_CONV - 1, r, inner)),
                  full((A_CONV, inner)), full((1, inner)),
                  full(prep["wqkv"].shape),
                  full((3 * inner, LANES)), full((8, 3 * inner)), full((1, LANES)), full((8, 1))],
        out_specs=[o[0] for o in outs],
        out_shape=[o[1] for o in outs],
        compiler_params=_cparams(("arbitrary",), 56),
        name="mlstm_pre_rows",
    )(xm, hist, prep["conv_w"], prep["conv_b"], prep["wqkv"],
      prep["wg"], prep["wgt"], prep["bg"], prep["bgt"])


def _p2_body(q_ref, k_ref, v_ref, gr_ref, h_ref, c_ref, n_ref, m_ref, caug, ms):
    c = pl.program_id(1)
    nc = pl.num_programs(1)
    L = q_ref.shape[1]
    heads = gr_ref.shape[1]
    hd = q_ref.shape[2] // heads

    @pl.when(c == 0)
    def _():
        caug[...] = jnp.zeros_like(caug)
        ms[...] = jnp.zeros_like(ms)

    row = lax.broadcasted_iota(I32, (L, L), 0)
    col = lax.broadcasted_iota(I32, (L, L), 1)
    causal = col <= row
    diag = col == row
    lane = lax.broadcasted_iota(I32, (L, LANES), 1)
    ones_col = jnp.where(lane == 0, 1.0, 0.0).astype(BF16)
    for a in range(heads):
        sl = slice(a * hd, (a + 1) * hd)
        q = q_ref[0, :, sl]
        k = k_ref[0, :, sl]
        v = v_ref[0, :, sl]
        gr = gr_ref[0, a]
        ig_r, lf_r = gr[0:1, :], gr[1:2, :]
        ig_c = jnp.sum(jnp.where(diag, ig_r, 0.0), axis=1, keepdims=True)
        lf_c = jnp.sum(jnp.where(diag, lf_r, 0.0), axis=1, keepdims=True)
        b_c = jnp.sum(jnp.where(causal, lf_r, 0.0), axis=1, keepdims=True)
        b_r = jnp.sum(jnp.where(row <= col, lf_c, 0.0), axis=0, keepdims=True)
        log_d = jnp.where(causal, b_c - b_r + ig_r, -jnp.inf)
        m_prev = ms[a]
        log_inter = b_c + m_prev
        m_t = jnp.maximum(log_inter, jnp.max(log_d, axis=1, keepdims=True))
        d = jnp.exp(log_d - m_t)
        w_inter = jnp.exp(log_inter - m_t)
        s = (_dot_nt(q, k) * d).astype(BF16)
        vaug = jnp.concatenate([v, ones_col], axis=1)
        ca = caug[a]
        num = w_inter * _dot(q, ca.astype(BF16)) + _dot(s, vaug)
        den = num[:, hd:hd + 1]
        h_ref[0, :, sl] = num[:, 0:hd] / jnp.maximum(jnp.abs(den), jnp.exp(-m_t))
        m_new = m_t[L - 1:L, :]
        b_last = b_c[L - 1:L, :]
        w_s = jnp.exp(b_last - b_c + ig_c - m_new)
        decay = jnp.exp(b_last + m_prev - m_new)
        kw = (k.astype(F32) * w_s).astype(BF16)
        caug[a] = decay * ca + _dot_tn(kw, vaug)
        ms[a] = m_new

    @pl.when(c == nc - 1)
    def _():
        for a in range(heads):
            c_ref[0, a] = caug[a, :, 0:hd]
            n_ref[0, a] = caug[a, :, hd:hd + 1]
            m_ref[0, a] = ms[a]


def _mlstm_scan(q, k, v, grow, L):
    b, s, inner = q.shape
    heads = grow.shape[1]
    hd = inner // heads
    qkv = pl.BlockSpec((1, L, inner), lambda i, c: (i, c, 0))
    return pl.pallas_call(
        _p2_body,
        grid=(b, s // L),
        in_specs=[qkv, qkv, qkv,
                  pl.BlockSpec((1, heads, 2, L), lambda i, c: (i, 0, 0, c))],
        out_specs=[qkv,
                   pl.BlockSpec((1, heads, hd, hd), lambda i, c: (i, 0, 0, 0)),
                   pl.BlockSpec((1, heads, hd, 1), lambda i, c: (i, 0, 0, 0)),
                   pl.BlockSpec((1, heads, 1, 1), lambda i, c: (i, 0, 0, 0))],
        out_shape=[jax.ShapeDtypeStruct((b, s, inner), F32),
                   jax.ShapeDtypeStruct((b, heads, hd, hd), F32),
                   jax.ShapeDtypeStruct((b, heads, hd, 1), F32),
                   jax.ShapeDtypeStruct((b, heads, 1, 1), F32)],
        scratch_shapes=[pltpu.VMEM((heads, hd, hd + LANES), F32), pltpu.VMEM((heads, 1, 1), F32)],
        compiler_params=_cparams(("parallel", "arbitrary")),
        name="mlstm_scan",
    )(q, k, v, grow)


def _s2_body(q_ref, k_ref, v_ref, g_ref, c_ref, n_ref, m_ref, h_ref, co_ref, no_ref, mo_ref):
    heads = c_ref.shape[1]
    hd = c_ref.shape[2]
    sub = lax.broadcasted_iota(I32, (8, hd), 0)
    first = sub == 0
    for a in range(heads):
        sl = slice(a * hd, (a + 1) * hd)
        q = q_ref[0, :, sl]
        k = k_ref[0, :, sl]
        v = v_ref[0, :, sl]
        g = g_ref[0, a]
        ig, lf = g[:, 0:1], g[:, 1:2]
        m_prev = m_ref[0, a]
        m_new = jnp.maximum(lf + m_prev, ig)
        decay = jnp.exp(lf + m_prev - m_new)
        dd = jnp.exp(ig - m_new)
        k8 = jnp.where(first, k, 0.0)
        v8 = jnp.where(first, v, 0.0)
        q8 = jnp.where(first, q, 0.0)
        c_new = decay * c_ref[0, a] + dd * _dot3(k8, v8, _dot_tn)
        n_new = decay * n_ref[0, a] + dd * k
        num = _dot3(q8, c_new)[0:1, :]
        den = jnp.sum(q * n_new, axis=1, keepdims=True)
        h_ref[0, :, sl] = num / jnp.maximum(jnp.abs(den), jnp.exp(-m_new))
        co_ref[0, a] = c_new
        no_ref[0, a] = n_new
        mo_ref[0, a] = m_new


def _mlstm_step(q, k, v, g4, c0, n0, m0):
    b, _, inner = q.shape
    heads = c0.shape[1]
    hd = inner // heads
    qkv = pl.BlockSpec((1, 1, inner), lambda i: (i, 0, 0))
    st = lambda r, c: pl.BlockSpec((1, heads, r, c), lambda i: (i, 0, 0, 0))
    return pl.pallas_call(
        _s2_body,
        grid=(b,),
        in_specs=[qkv, qkv, qkv, st(1, 2), st(hd, hd), st(1, hd), st(1, 1)],
        out_specs=[qkv, st(hd, hd), st(1, hd), st(1, 1)],
        out_shape=[jax.ShapeDtypeStruct((b, 1, inner), F32),
                   jax.ShapeDtypeStruct((b, heads, hd, hd), F32),
                   jax.ShapeDtypeStruct((b, heads, 1, hd), F32),
                   jax.ShapeDtypeStruct((b, heads, 1, 1), F32)],
        compiler_params=_cparams(("parallel",)),
        name="mlstm_step",
    )(q, k, v, g4, c0, n0, m0)


def _route(logits):
    ts = logits.shape[0]
    lane = lax.broadcasted_iota(I32, (ts, LANES), 1)
    neg = -jnp.inf
    big = jnp.int32(1 << 20)
    is_g = lane < MOE_GROUPS
    gl = jnp.where(is_g, logits, neg)
    gmax = jnp.max(gl, axis=1, keepdims=True)
    gidx = jnp.min(jnp.where(gl == gmax, lane, big), axis=1, keepdims=True)
    gsum = jnp.sum(jnp.where(is_g, jnp.exp(logits - gmax), 0.0), axis=1, keepdims=True)
    g_w = 1.0 / gsum
    lo = MOE_GROUPS + MOE_EXPERTS * gidx
    el = jnp.where(lane >= lo, jnp.where(lane < lo + MOE_EXPERTS, logits, neg), neg)
    v1 = jnp.max(el, axis=1, keepdims=True)
    i1 = jnp.min(jnp.where(el == v1, lane, big), axis=1, keepdims=True)
    el2 = jnp.where(lane == i1, neg, el)
    v2 = jnp.max(el2, axis=1, keepdims=True)
    i2 = jnp.min(jnp.where(el2 == v2, lane, big), axis=1, keepdims=True)
    t = jnp.exp(v2 - v1)
    w1 = g_w / (1.0 + t)
    w2 = g_w * t / (1.0 + t)
    e1 = (i1 - MOE_GROUPS).astype(F32)
    e2 = (i2 - MOE_GROUPS).astype(F32)
    return jnp.where(lane == 0, e1, jnp.where(lane == 1, e2, jnp.where(lane == 2, w1, jnp.where(lane == 3, w2, 0.0))))


def _epilogue(x, o, g1, sc2, sh2, lng, lnb, wrh, wrl, br, x1_ref, h2_ref, ri_ref):
    x1 = _layer_norm(DN_ALPHA * x + (1.0 + g1) * o, lng, lnb)
    x1_ref[0] = x1
    h2 = x1 * (1.0 + sc2) + sh2
    h2_ref[0] = _pack_pairs(h2) if h2_ref.dtype == U32 else h2
    xh = h2.astype(BF16)
    xl = (h2 - xh.astype(F32)).astype(BF16)
    logits = _dot(xh, wrh) + _dot(xl, wrh) + _dot(xh, wrl) + br
    ri_ref[0] = _route(logits)


def _mixa_body(h_ref, xc_ref, z_ref, x_ref, g1_ref, sc2_ref, sh2_ref, nw_ref, skip_ref, wo_ref,
               lng_ref, lnb_ref, wrh_ref, wrl_ref, br_ref, x1_ref, h2_ref, ri_ref, *, heads):
    h = h_ref[0]
    inner = h.shape[1]
    hd = inner // heads
    parts = []
    for a in range(heads):
        ha = h[:, a * hd:(a + 1) * hd]
        hc = ha - jnp.mean(ha, axis=1, keepdims=True)
        parts.append(hc * lax.rsqrt(jnp.mean(hc * hc, axis=1, keepdims=True) + LN_EPS))
    hn = jnp.concatenate(parts, axis=1) * nw_ref[...]
    z = z_ref[0]
    out = (hn + skip_ref[...] * xc_ref[0]) * (1.0 / (1.0 + jnp.exp(-z)))
    o = _mm(out, wo_ref[...])
    _epilogue(x_ref[0], o, g1_ref[0], sc2_ref[0], sh2_ref[0], lng_ref[...], lnb_ref[...],
              wrh_ref[...], wrl_ref[...], br_ref[...], x1_ref, h2_ref, ri_ref)


def _mixb_body(o_ref, x_ref, g1_ref, sc2_ref, sh2_ref, sw_ref, wo_ref,
               lng_ref, lnb_ref, wrh_ref, wrl_ref, br_ref, x1_ref, h2_ref, ri_ref, *, heads, gain):
    o = o_ref[0]
    vd = o.shape[1] // heads
    parts = []
    for a in range(heads):
        oa = o[:, a * vd:(a + 1) * vd]
        parts.append(oa * lax.rsqrt(jnp.mean(oa * oa, axis=1, keepdims=True) + LN_EPS) * sw_ref[...] * gain)
    on = jnp.concatenate(parts, axis=1)
    y = _mm(on, wo_ref[...])
    _epilogue(x_ref[0], y, g1_ref[0], sc2_ref[0], sh2_ref[0], lng_ref[...], lnb_ref[...],
              wrh_ref[...], wrl_ref[...], br_ref[...], x1_ref, h2_ref, ri_ref)


def _mixer_out(body, acts, x, mod, consts, ts, packed, name):
    b, s, d = x.shape
    hw, hdt = (d // 2, U32) if packed else (d, F32)
    rowspec = lambda a: pl.BlockSpec((1, ts, a.shape[2]), lambda i, j: (i, j, 0))
    full = lambda a: pl.BlockSpec(a.shape, lambda i, j: (0,) * a.ndim)
    in_specs = [rowspec(a) for a in acts] + [rowspec(x)]
    in_specs += [_mod_spec(mod, ts, 2), _mod_spec(mod, ts, 4), _mod_spec(mod, ts, 3)]
    in_specs += [full(c) for c in consts]
    return pl.pallas_call(
        body,
        grid=(b, s // ts),
        in_specs=in_specs,
        out_specs=[pl.BlockSpec((1, ts, d), lambda i, j: (i, j, 0)),
                   pl.BlockSpec((1, ts, hw), lambda i, j: (i, j, 0)),
                   pl.BlockSpec((1, ts, LANES), lambda i, j: (i, j, 0))],
        out_shape=[jax.ShapeDtypeStruct((b, s, d), F32),
                   jax.ShapeDtypeStruct((b, s, hw), hdt),
                   jax.ShapeDtypeStruct((b, s, LANES), F32)],
        compiler_params=_cparams(("parallel", "parallel"), 56),
        name=name,
    )(*acts, x, mod, mod, mod, *consts)


def _sort_body(e1_ref, e2_ref, pa_ref, pb_ref, cnt_ref, *, tmx, cw):
    nb_tok = e1_ref.shape[2]
    e1 = e1_ref[0]
    e2 = e2_ref[0]
    sub = lax.broadcasted_iota(I32, (N_EXPERTS, nb_tok), 0)
    a_hot = sub == e1
    b_hot = sub == e2
    m = jnp.where(a_hot, 1.0, 0.0) + jnp.where(b_hot, 1.0, 0.0)
    cnt = jnp.sum(m, axis=1, keepdims=True)
    padded = jnp.floor((cnt + (tmx - 1)) / tmx) * tmx
    r16 = lax.broadcasted_iota(I32, (N_EXPERTS, N_EXPERTS), 0)
    c16 = lax.broadcasted_iota(I32, (N_EXPERTS, N_EXPERTS), 1)
    prow = jnp.sum(jnp.where(r16 == c16, padded, 0.0), axis=0, keepdims=True)
    segoff = jnp.sum(jnp.where(c16 < r16, prow, 0.0), axis=1, keepdims=True)
    cnt_ref[0] = jnp.broadcast_to(cnt, (N_EXPERTS, LANES)).astype(I32)
    ur = lax.broadcasted_iota(I32, (cw, cw), 0)
    uc = lax.broadcasted_iota(I32, (cw, cw), 1)
    upper = jnp.where(ur < uc, 1.0, 0.0).astype(BF16)
    carry = segoff
    for j in range(nb_tok // cw):
        sl = slice(j * cw, (j + 1) * cw)
        mc = m[:, sl]
        rank = _dot(mc.astype(BF16), upper) + carry
        pa_ref[0, :, sl] = jnp.sum(jnp.where(a_hot[:, sl], rank, 0.0), axis=0, keepdims=True).astype(I32)
        pb_ref[0, :, sl] = jnp.sum(jnp.where(b_hot[:, sl], rank, 0.0), axis=0, keepdims=True).astype(I32)
        carry = carry + jnp.sum(mc, axis=1, keepdims=True)


def _moe_sort(e1, e2, tmx):
    nb, _, nbt = e1.shape
    cw = min(nbt, 256)
    tok = pl.BlockSpec((1, 1, nbt), lambda i: (i, 0, 0))
    return pl.pallas_call(
        functools.partial(_sort_body, tmx=tmx, cw=cw),
        grid=(nb,),
        in_specs=[tok, tok],
        out_specs=[tok, tok, pl.BlockSpec((1, N_EXPERTS, LANES), lambda i: (i, 0, 0))],
        out_shape=[jax.ShapeDtypeStruct((nb, 1, nbt), I32), jax.ShapeDtypeStruct((nb, 1, nbt), I32),
                   jax.ShapeDtypeStruct((nb, N_EXPERTS, LANES), I32)],
        compiler_params=_cparams(("parallel",)),
        name="moe_sort",
    )(e1, e2)


def _dispatch_body(pa_ref, pb_ref, zs_ref, zn_ref, src_ref, dst_ref):
    nbt = src_ref.shape[1]
    zero_row = jnp.zeros((1, dst_ref.shape[2]), dst_ref.dtype)
    zero_tile = jnp.zeros((8, dst_ref.shape[2]), dst_ref.dtype)

    for e in range(N_EXPERTS):
        start = zs_ref[0, 0, e]

        def clear(i, carry, start=start):
            dst_ref[0, pl.ds(start + i, 1), :] = zero_row
            return carry

        lax.fori_loop(0, zn_ref[0, 0, e], clear, 0)

    tail = zs_ref[0, 0, N_EXPERTS]

    def clear_tail(i, carry):
        dst_ref[0, pl.ds(pl.multiple_of(tail + 8 * i, 8), 8), :] = zero_tile
        return carry

    lax.fori_loop(0, zn_ref[0, 0, N_EXPERTS], clear_tail, 0)

    def body(i, carry):
        row = src_ref[0, pl.ds(i, 1), :]
        dst_ref[0, pl.ds(pa_ref[0, 0, i], 1), :] = row
        dst_ref[0, pl.ds(pb_ref[0, 0, i], 1), :] = row
        return carry

    lax.fori_loop(0, nbt, body, 0, unroll=min(8, nbt))


def _moe_dispatch(pa, pb, zstart, zlen, h2p, rp):
    nb, nbt, w = h2p.shape
    smem = pl.BlockSpec((1, 1, nbt), lambda i: (i, 0, 0), memory_space=pltpu.SMEM)
    seg = pl.BlockSpec((1, 1, N_EXPERTS + 1), lambda i: (i, 0, 0), memory_space=pltpu.SMEM)
    return pl.pallas_call(
        _dispatch_body,
        grid=(nb,),
        in_specs=[smem, smem, seg, seg, pl.BlockSpec((1, nbt, w), lambda i: (i, 0, 0))],
        out_specs=pl.BlockSpec((1, rp, w), lambda i: (i, 0, 0), pipeline_mode=pl.Buffered(1)),
        out_shape=jax.ShapeDtypeStruct((nb, rp, w), h2p.dtype),
        compiler_params=_cparams(("arbitrary",), 56),
        name="moe_dispatch",
    )(pa, pb, zstart, zlen, h2p)


def _expert_body(tb_ref, tr_ref, te_ref, tv_ref, tf_ref, x_ref, w1_ref, w3_ref, w2_ref, o_ref, w1b, w3b, w2b):
    t = pl.program_id(0)

    @pl.when(tf_ref[t] == 1)
    def _():
        w1b[...] = w1_ref[0].astype(BF16)
        w3b[...] = w3_ref[0].astype(BF16)
        w2b[...] = w2_ref[0].astype(BF16)

    @pl.when(tv_ref[t] == 1)
    def _():
        if x_ref.dtype == U32:
            lo, hi = _unpack_pairs(x_ref[0])
            lo = lo.astype(BF16)
            hi = hi.astype(BF16)
            half = lo.shape[1]
            a = _dot(lo, w1b[0:half, :]) + _dot(hi, w1b[half:2 * half, :])
            g = _dot(lo, w3b[0:half, :]) + _dot(hi, w3b[half:2 * half, :])
        else:
            xb = x_ref[0].astype(BF16)
            a = _dot(xb, w1b[...])
            g = _dot(xb, w3b[...])
        y = _dot((_silu(a) * g).astype(BF16), w2b[...])
        o_ref[0] = _pack_pairs(y) if o_ref.dtype == U32 else y


def _moe_experts(tables, xs, w1, w3, w2, tmx):
    nb, rp, half = xs.shape
    ne, d, ff = w1.shape
    nt = tables[0].shape[0]
    xspec = pl.BlockSpec((1, tmx, half), lambda t, tb, tr, te, tv, tf: (tb[t], tr[t], 0))
    grid_spec = pltpu.PrefetchScalarGridSpec(
        num_scalar_prefetch=5,
        grid=(nt,),
        in_specs=[xspec,
                  pl.BlockSpec((1, d, ff), lambda t, tb, tr, te, tv, tf: (te[t], 0, 0)),
                  pl.BlockSpec((1, d, ff), lambda t, tb, tr, te, tv, tf: (te[t], 0, 0)),
                  pl.BlockSpec((1, ff, d), lambda t, tb, tr, te, tv, tf: (te[t], 0, 0))],
        out_specs=xspec,
        scratch_shapes=[pltpu.VMEM((d, ff), BF16), pltpu.VMEM((d, ff), BF16), pltpu.VMEM((ff, d), BF16)],
    )
    return pl.pallas_call(
        _expert_body,
        grid_spec=grid_spec,
        out_shape=jax.ShapeDtypeStruct((nb, rp, half), xs.dtype),
        input_output_aliases={len(tables): 0},
        compiler_params=_cparams(("arbitrary",)),
        name="moe_experts",
    )(*tables, xs, w1, w3, w2)


def _combine_body(pa_ref, pb_ref, ri_ref, ys_ref, x1_ref, g2_ref, lng_ref, lnb_ref, o_ref, ga, gb):
    tc = x1_ref.shape[1]

    def body(i, carry):
        ga[pl.ds(i, 1), :] = ys_ref[0, pl.ds(pa_ref[0, 0, i], 1), :]
        gb[pl.ds(i, 1), :] = ys_ref[0, pl.ds(pb_ref[0, 0, i], 1), :]
        return carry

    lax.fori_loop(0, tc, body, 0, unroll=min(8, tc))
    ri = ri_ref[0]
    wa = ri[:, 2:3]
    wb = ri[:, 3:4]
    if ga.dtype == U32:
        alo, ahi = _unpack_pairs(ga[...])
        blo, bhi = _unpack_pairs(gb[...])
        ffn = jnp.concatenate([wa * alo + wb * blo, wa * ahi + wb * bhi], axis=1)
    else:
        ffn = wa * ga[...] + wb * gb[...]
    y = DN_ALPHA * x1_ref[0] + (1.0 + g2_ref[0]) * ffn
    o_ref[0] = _layer_norm(y, lng_ref[...], lnb_ref[...])


def _moe_combine(pa, pb, ri, ys, x1, mod, lng, lnb, tc):
    nb, nbt, d = x1.shape
    rp, half = ys.shape[1:]
    smem = pl.BlockSpec((1, 1, tc), lambda i, j: (i, 0, j), memory_space=pltpu.SMEM)
    per_mod = (nb * nbt) // mod.shape[0]
    assert mod.shape[1] == 1 and per_mod % tc == 0
    return pl.pallas_call(
        _combine_body,
        grid=(nb, nbt // tc),
        in_specs=[smem, smem,
                  pl.BlockSpec((1, tc, LANES), lambda i, j: (i, j, 0)),
                  pl.BlockSpec((1, rp, half), lambda i, j: (i, 0, 0), pipeline_mode=pl.Buffered(1)),
                  pl.BlockSpec((1, tc, d), lambda i, j: (i, j, 0)),
                  pl.BlockSpec((1, 1, d), lambda i, j: ((i * nbt + j * tc) // per_mod, 0, 5)),
                  pl.BlockSpec((1, d), lambda i, j: (0, 0)),
                  pl.BlockSpec((1, d), lambda i, j: (0, 0))],
        out_specs=pl.BlockSpec((1, tc, d), lambda i, j: (i, j, 0)),
        out_shape=jax.ShapeDtypeStruct((nb, nbt, d), F32),
        scratch_shapes=[pltpu.VMEM((tc, half), ys.dtype), pltpu.VMEM((tc, half), ys.dtype)],
        compiler_params=_cparams(("parallel", "arbitrary"), 56),
        name="moe_combine",
    )(pa, pb, ri, ys, x1, mod, lng, lnb)


def _tile_tables(cnt, tmx, max_tiles):
    nb = cnt.shape[0]
    nt = (cnt + (tmx - 1)) // tmx
    first_tile = jnp.cumsum(nt, axis=1) - nt
    nt_eb = nt.T.reshape(-1)
    ends = jnp.cumsum(nt_eb)
    total = ends[-1]
    t = jnp.arange(max_tiles, dtype=I32)
    valid = t < total
    tc = jnp.minimum(t, total - 1)
    seg = jnp.sum((ends[None, :] <= tc[:, None]).astype(I32), axis=1)
    within = tc - (ends[seg] - nt_eb[seg])
    te = seg // nb
    tb = seg % nb
    tr = first_tile[tb, te] + within
    prev = jnp.concatenate([jnp.full((1,), -1, I32), te[:-1]])
    tf = (valid & (te != prev)).astype(I32)
    return tb.astype(I32), tr.astype(I32), te.astype(I32), valid.astype(I32), tf


def _moe(h2p, ri, x1, mod, w1, w3, w2, lng, lnb, nbt, tmx, tc):
    b, s, d = x1.shape
    nb = (b * s) // nbt
    rp = 2 * nbt + N_EXPERTS * tmx
    max_tiles = nb * (2 * nbt // tmx + N_EXPERTS)
    r = ri.reshape(nb, nbt, LANES)
    e1 = r[:, :, 0].astype(I32).reshape(nb, 1, nbt)
    e2 = r[:, :, 1].astype(I32).reshape(nb, 1, nbt)
    pa, pb, cnt = _moe_sort(e1, e2, tmx)
    tables = _tile_tables(cnt[:, :, 0], tmx, max_tiles)
    cnt16 = cnt[:, :, 0]
    padded = (cnt16 + (tmx - 1)) // tmx * tmx
    used = jnp.sum(padded, axis=1, keepdims=True)
    zstart = jnp.concatenate([jnp.cumsum(padded, axis=1) - padded + cnt16, used], axis=1).reshape(nb, 1, -1)
    zlen = jnp.concatenate([padded - cnt16, (rp - used) // 8], axis=1).reshape(nb, 1, -1)
    xs = _moe_dispatch(pa, pb, zstart, zlen, h2p.reshape(nb, nbt, h2p.shape[-1]), rp)
    ys = _moe_experts(tables, xs, w1, w3, w2, tmx)
    x2 = _moe_combine(pa, pb, r, ys, x1.reshape(nb, nbt, d), mod, lng, lnb, tc)
    return x2.reshape(b, s, d)


def _moe_dense_body(h_ref, ri_ref, x1_ref, g2_ref, lng_ref, lnb_ref, w1_ref, w3_ref, w2_ref, o_ref, acc):
    e = pl.program_id(0)

    @pl.when(e == 0)
    def _():
        acc[...] = jnp.zeros_like(acc)

    x = h_ref[0]
    ri = ri_ref[0]
    ef = e.astype(F32)
    comb = jnp.where(ri[:, 0:1] == ef, ri[:, 2:3], 0.0) + jnp.where(ri[:, 1:2] == ef, ri[:, 3:4], 0.0)
    hact = _silu(_dot3(x, w1_ref[0])) * _dot3(x, w3_ref[0]) * comb
    acc[...] += _dot3(hact, w2_ref[0])

    @pl.when(e == pl.num_programs(0) - 1)
    def _():
        y = DN_ALPHA * x1_ref[0] + (1.0 + g2_ref[0]) * acc[...]
        o_ref[0] = _layer_norm(y, lng_ref[...], lnb_ref[...])


def _moe_dense(h2, ri, x1, mod, w1, w3, w2, lng, lnb):
    _, r, d = x1.shape
    ne, _, ff = w1.shape
    row = lambda n: pl.BlockSpec((1, r, n), lambda e: (0, 0, 0))
    return pl.pallas_call(
        _moe_dense_body,
        grid=(ne,),
        in_specs=[row(d), row(LANES), row(d),
                  pl.BlockSpec((1, r, d), lambda e: (0, 0, 5)),
                  pl.BlockSpec((1, d), lambda e: (0, 0)), pl.BlockSpec((1, d), lambda e: (0, 0)),
                  pl.BlockSpec((1, d, ff), lambda e: (e, 0, 0)), pl.BlockSpec((1, d, ff), lambda e: (e, 0, 0)),
                  pl.BlockSpec((1, ff, d), lambda e: (e, 0, 0))],
        out_specs=row(d),
        out_shape=jax.ShapeDtypeStruct((1, r, d), F32),
        scratch_shapes=[pltpu.VMEM((r, d), F32)],
        compiler_params=_cparams(("arbitrary",)),
        name="moe_dense",
    )(h2, ri, x1, mod, lng, lnb, w1, w3, w2)


def _lambda(lam_ref):
    lp = lam_ref[...]
    a = jnp.sum(lp[0:1, :] * lp[1:2, :], axis=1, keepdims=True)
    c = jnp.sum(lp[2:3, :] * lp[3:4, :], axis=1, keepdims=True)
    return jnp.exp(a) - jnp.exp(c)


POS_SPLIT = 16


ONES_ROWS = 16


def _attn_body(slopes_ref, q_ref, k_ref, v_ref, lam_ref, o_ref, kaug, vt, qaug, m_s, acc, sbuf, *, lam_init, vd):
    hg = pl.program_id(1)
    j = pl.program_id(2)
    tq = q_ref.shape[1]
    group = q_ref.shape[2] // vd
    s_len = k_ref.shape[1]
    tk = tq

    @pl.when(j == 0)
    def _():
        pos = lax.broadcasted_iota(I32, (s_len, LANES), 0)
        ln = lax.broadcasted_iota(I32, (s_len, LANES), 1)
        hi = (pos // POS_SPLIT).astype(F32)
        lo = (pos % POS_SPLIT).astype(F32)
        ktail = jnp.where(ln == 0, hi, jnp.where(ln == 1, lo, 0.0)).astype(BF16)
        orow = lax.broadcasted_iota(I32, (ONES_ROWS, s_len), 0)
        ones_rows = jnp.where(orow == 0, 1.0, 0.0).astype(BF16)
        for g in range(group):
            kaug[g, :, 0:vd] = k_ref[0, :, g * vd:(g + 1) * vd].astype(BF16)
            kaug[g, :, vd:vd + LANES] = ktail
            vt[g, 0:vd, :] = v_ref[0, :, g * vd:(g + 1) * vd].T.astype(BF16)
            vt[g, vd:vd + ONES_ROWS, :] = ones_rows

    lane = lax.broadcasted_iota(I32, (tq, vd), 1)
    for g in range(group):
        slope = slopes_ref[hg * group + g]
        q = q_ref[0, :, g * vd:(g + 1) * vd].astype(F32)
        extra = jnp.where(lane == 0, slope * POS_SPLIT, jnp.where(lane == 1, slope, 0.0)).astype(BF16)
        qaug[g, 0:tq, 0:vd] = jnp.where(lane < vd // 2, q, 0.0).astype(BF16)
        qaug[g, 0:tq, vd:vd + LANES] = extra
        qaug[g, tq:2 * tq, 0:vd] = jnp.where(lane >= vd // 2, q, 0.0).astype(BF16)
        qaug[g, tq:2 * tq, vd:vd + LANES] = extra
    m_s[...] = jnp.full_like(m_s, -jnp.inf)
    acc[...] = jnp.zeros_like(acc)

    def scores(kb):
        off = pl.multiple_of(kb * tk, tk)
        return [_dot_nt(kaug[g, pl.ds(off, tk), :], qaug[g]) for g in range(group)]

    def consume(kb, masked):
        off = pl.multiple_of(kb * tk, tk)
        for g in range(group):
            st = sbuf[g]
            if masked:
                r = lax.broadcasted_iota(I32, st.shape, 0)
                c = lax.broadcasted_iota(I32, st.shape, 1)
                st = jnp.where(r <= (c & (tq - 1)), st, -jnp.inf)
            m_old = m_s[g]
            m_new = jnp.maximum(m_old, jnp.max(st, axis=0, keepdims=True))
            a = jnp.exp(m_old - m_new)
            p = jnp.exp(st - m_new)
            acc[g] = a * acc[g] + _dot(vt[g, :, pl.ds(off, tk)], p.astype(BF16))
            m_s[g] = m_new

    def put(sts):
        for g in range(group):
            sbuf[g] = sts[g]

    def off_diagonal(kb, carry):
        nxt = scores(kb + 1)
        consume(kb, False)
        put(nxt)
        return carry

    put(scores(0))
    lax.fori_loop(0, j, off_diagonal, 0)
    consume(j, True)
    lam = _lambda(lam_ref) + lam_init
    for g in range(group):
        num = acc[g, 0:vd, :]
        den = acc[g, vd:vd + 1, :]
        ot = num[:, 0:tq] / den[:, 0:tq] - lam * (num[:, tq:2 * tq] / den[:, tq:2 * tq])
        o_ref[0, :, g * vd:(g + 1) * vd] = ot.T


def _attn_prompt(q, k, v, lam_p, slopes, lam_init, tq, group):
    b, s, d = q.shape
    heads = slopes.shape[0]
    vd = d // heads
    gw = group * vd
    assert heads % group == 0 and tq & (tq - 1) == 0
    assert s // POS_SPLIT <= 256
    grid_spec = pltpu.PrefetchScalarGridSpec(
        num_scalar_prefetch=1,
        grid=(b, heads // group, s // tq),
        in_specs=[pl.BlockSpec((1, tq, gw), lambda i, h, j, sl: (i, j, h)),
                  pl.BlockSpec((1, s, gw), lambda i, h, j, sl: (i, 0, h)),
                  pl.BlockSpec((1, s, gw), lambda i, h, j, sl: (i, 0, h)),
                  pl.BlockSpec(lam_p.shape, lambda i, h, j, sl: (0, 0))],
        out_specs=pl.BlockSpec((1, tq, gw), lambda i, h, j, sl: (i, j, h)),
        scratch_shapes=[pltpu.VMEM((group, s, vd + LANES), BF16), pltpu.VMEM((group, vd + ONES_ROWS, s), BF16),
                        pltpu.VMEM((group, 2 * tq, vd + LANES), BF16),
                        pltpu.VMEM((group, 1, 2 * tq), F32),
                        pltpu.VMEM((group, vd + ONES_ROWS, 2 * tq), F32),
                        pltpu.VMEM((group, tq, 2 * tq), F32)],
    )
    return pl.pallas_call(
        functools.partial(_attn_body, lam_init=lam_init, vd=vd),
        grid_spec=grid_spec,
        out_shape=jax.ShapeDtypeStruct((b, s, d), F32),
        compiler_params=_cparams(("parallel", "parallel", "arbitrary"), 56),
        name="attn_prompt",
    )(slopes, q, k, v, lam_p)


def _decode_body(pt_ref, q_ref, kn_ref, vn_ref, slope_ref, lam_ref, *rest, lam_init, heads):
    npg = PAGES_PER_STEP
    k_refs = rest[:npg]
    v_refs = rest[npg:2 * npg]
    o_ref, m_s, l_s, acc, mb = rest[2 * npg:]
    j = pl.program_id(1)
    nj = pl.num_programs(1)
    vd = q_ref.shape[3]
    rows = 2 * heads
    keys = PAGE_SIZE * heads
    step_keys = npg * PAGE_SIZE
    lane = lax.broadcasted_iota(I32, (heads, vd), 1)
    qf = q_ref[0, 0].astype(F32)
    qm = jnp.concatenate([jnp.where(lane < vd // 2, qf, 0.0), jnp.where(lane >= vd // 2, qf, 0.0)], axis=0)
    shift = slope_ref[...] * float(step_keys)

    @pl.when(j == 0)
    def _():
        kn = kn_ref[0, 0]
        vn = vn_ref[0, 0]
        s_self = jnp.sum(qm * jnp.concatenate([kn, kn], axis=0), axis=1, keepdims=True)
        m_s[...] = s_self + shift * (1.0 * nj)
        l_s[...] = jnp.ones_like(l_s)
        acc[...] = jnp.concatenate([vn, vn], axis=0)
        r_i = lax.broadcasted_iota(I32, mb.shape, 0)
        c_i = lax.broadcasted_iota(I32, mb.shape, 1)
        dist = (step_keys - c_i // heads).astype(F32)
        mb[...] = jnp.where((c_i % heads) == (r_i % heads), -slope_ref[...] * dist, -jnp.inf)

    qh, ql = _split(qm)
    qhl = jnp.concatenate([qh, ql], axis=0)

    def scores(i):
        both = _dot_nt(qhl, k_refs[i][0].reshape(keys, vd).astype(BF16))
        return both[0:rows] + both[rows:2 * rows]

    s = jnp.concatenate([scores(i) for i in range(npg)], axis=1) + mb[...]
    m_old = m_s[...] - shift
    m_new = jnp.maximum(m_old, jnp.max(s, axis=1, keepdims=True))
    a = jnp.exp(m_old - m_new)
    p = jnp.exp(s - m_new)
    l_s[...] = a * l_s[...] + jnp.sum(p, axis=1, keepdims=True)
    ph, plo = _split(p)
    phl = jnp.concatenate([ph, plo], axis=0)
    pv = jnp.zeros((rows, vd), F32)
    for i in range(npg):
        both = _dot(phl[:, i * keys:(i + 1) * keys], v_refs[i][0].reshape(keys, vd).astype(BF16))
        pv = pv + both[0:rows] + both[rows:2 * rows]
    acc[...] = a * acc[...] + pv
    m_s[...] = m_new

    @pl.when(j == nj - 1)
    def _():
        lam = _lambda(lam_ref) + lam_init
        o16 = acc[...] / l_s[...]
        o_ref[0, 0] = o16[0:heads, :] - lam * o16[heads:rows, :]


def _attn_decode(q, k_new, v_new, cache_k, cache_v, page_table, lam_p, slopes, lam_init):
    db, _, heads, vd = q.shape
    n_pages = page_table.shape[1]
    npg = PAGES_PER_STEP
    assert n_pages % npg == 0
    rows = 2 * heads
    slope16 = jnp.concatenate([slopes, slopes]).reshape(rows, 1)
    tok = pl.BlockSpec((1, 1, heads, vd), lambda i, j, pt: (i, 0, 0, 0))

    def page_spec(idx):
        return pl.BlockSpec((1, PAGE_SIZE, heads, vd), lambda i, j, pt: (pt[i, j * npg + idx], 0, 0, 0))

    grid_spec = pltpu.PrefetchScalarGridSpec(
        num_scalar_prefetch=1,
        grid=(db, n_pages // npg),
        in_specs=[tok, tok, tok,
                  pl.BlockSpec((rows, 1), lambda i, j, pt: (0, 0)),
                  pl.BlockSpec(lam_p.shape, lambda i, j, pt: (0, 0))]
                 + [page_spec(i) for i in range(npg)] + [page_spec(i) for i in range(npg)],
        out_specs=tok,
        scratch_shapes=[pltpu.VMEM((rows, 1), F32), pltpu.VMEM((rows, 1), F32), pltpu.VMEM((rows, vd), F32),
                        pltpu.VMEM((rows, npg * PAGE_SIZE * heads), F32)],
    )
    return pl.pallas_call(
        functools.partial(_decode_body, lam_init=lam_init, heads=heads),
        grid_spec=grid_spec,
        out_shape=jax.ShapeDtypeStruct((db, 1, heads, vd), F32),
        compiler_params=_cparams(("parallel", "arbitrary")),
        name="attn_decode",
    )(page_table, q, k_new, v_new, slope16, lam_p, *([cache_k] * npg), *([cache_v] * npg))


MXU_WIDTH = 256


def _headwise_blocks(w):
    rows = w.reshape(-1, MXU_WIDTH, A_QKV_BLOCK)
    tiled = jnp.tile(rows, (1, 1, MXU_WIDTH // A_QKV_BLOCK))
    r = lax.broadcasted_iota(I32, tiled.shape, 1) // A_QKV_BLOCK
    s = lax.broadcasted_iota(I32, tiled.shape, 2) // A_QKV_BLOCK
    return jnp.where(r == s, tiled, 0.0)


def _prep_mlstm(conv_w, conv_b, w_q, w_k, w_v, w_gate, b_gate, heads, dt):
    inner = conv_w.shape[1]
    hd = inner // heads
    ng = b_gate.shape[0]
    wg = w_gate.reshape(heads, 3, hd, ng).transpose(1, 0, 2, 3).reshape(3 * inner, ng)
    wgp = jnp.pad(wg, ((0, 0), (0, LANES - ng)))
    return dict(conv_w=conv_w, conv_b=conv_b.reshape(1, inner),
                wqkv=jnp.stack([_headwise_blocks(w_q), _headwise_blocks(w_k) * hd ** -0.5,
                                _headwise_blocks(w_v)]).astype(dt),
                wg=wgp.astype(dt), wgt=wg.T.astype(BF16),
                bg=jnp.pad(b_gate, (0, LANES - ng)).reshape(1, LANES), bgt=b_gate.reshape(ng, 1))


def _prep_router(w_group, b_group, w_expert, b_expert):
    d = w_group.shape[0]
    we = w_expert.transpose(1, 0, 2).reshape(d, N_EXPERTS)
    wr = jnp.pad(jnp.concatenate([w_group, we], axis=1), ((0, 0), (0, LANES - MOE_GROUPS - N_EXPERTS)))
    br = jnp.pad(jnp.concatenate([b_group, b_expert.reshape(-1)]), (0, LANES - MOE_GROUPS - N_EXPERTS))
    wrh = wr.astype(BF16)
    wrl = (wr - wrh.astype(F32)).astype(BF16)
    return wrh, wrl, br.reshape(1, LANES)


def _trunk(x, mod, conv0, state, cache, wts, cfg):
    b, s, d = x.shape
    heads_a = wts["heads_a"]
    ts = cfg["ts"]
    row_shape = cfg["row_shape"]
    xr = x.reshape(row_shape + (d,))

    m0 = mod[0]
    tn = cfg["tn"]
    xm, z = _linear(xr, wts["a_w_in"], ts, (F32, F32), mod=m0, cols=(1, 0), tn=tn, name="a_in")
    inner = xm.shape[-1]
    hd = inner // heads_a
    if cfg["chunk"] is not None:
        tail8 = jnp.pad(conv0, ((0, 0), (8 - (A_CONV - 1), 0), (0, 0)))
        xc, q, k, v, gt, conv8 = _mlstm_pre(xm, tail8, wts["a_prep"], cfg["ts_pre"])
        conv_new = conv8[:, 8 - (A_CONV - 1):, :]
        grow = gt.reshape(b, 2, heads_a, s).transpose(0, 2, 1, 3)
        hh, c_new, n_new, m_new = _mlstm_scan(q, k, v, grow, cfg["chunk"])
    else:
        xc, q, k, v, g = _mlstm_pre_rows(xm, conv0.transpose(1, 0, 2), wts["a_prep"])
        conv_new = jnp.concatenate([conv0[:, 1:, :], xm.reshape(b, s, inner)], axis=1)
        c0, n0, mm0 = state
        g4 = g.reshape(b, 2, heads_a).transpose(0, 2, 1).reshape(b, heads_a, 1, 2)
        tok = lambda a: a.reshape(b, s, inner)
        hh, c_new, n_new, m_new = _mlstm_step(tok(q), tok(k), tok(v), g4, c0, n0.reshape(b, heads_a, 1, hd),
                                              mm0.reshape(b, heads_a, 1, 1))
    n_new = n_new.reshape(b, heads_a, hd)
    m_new = m_new.reshape(b, heads_a)
    rs = lambda a: a.reshape(row_shape + (a.shape[-1],))
    x1, h2p, ri = _mixer_out(
        functools.partial(_mixa_body, heads=heads_a), [rs(hh), rs(xc), z], xr, m0,
        [wts["a_norm_w"], wts["a_skip"], wts["a_w_out"], wts["ln_g"][0][0], wts["ln_b"][0][0], *wts["router"][0]],
        cfg["ts_out"], cfg["packed"], "mix_a")

    def moe(h2p, ri, xres, m, layer):
        lng, lnb = wts["ln_g"][layer][1], wts["ln_b"][layer][1]
        if cfg["dense_moe"]:
            return _moe_dense(h2p, ri, xres, m, *wts["moe"][layer], lng, lnb)
        return _moe(h2p, ri, xres, m, *wts["moe"][layer], lng, lnb, cfg["nbt"], cfg["tmx"], cfg["tc"])

    x2 = moe(h2p, ri, x1, m0, 0)

    k_sh, v_sh = _linear(x2, wts["kv_w"], ts, (F32, F32), tn=tn, name="kv_proj")

    m1 = mod[1]
    lam_init = 0.8 - 0.6 * math.exp(-0.3 * 1)
    (qq,) = _linear(x2, wts["b_w_q"], ts, (cfg["q_dtype"],), mod=m1, cols=(1, 0), out_scale=wts["q_scale"],
                    tn=tn, name="q_proj")
    if cache is None:
        o = _attn_prompt(qq.reshape(b, s, d), k_sh.reshape(b, s, d), v_sh.reshape(b, s, d),
                         wts["b_lambda"], wts["slopes"], lam_init, cfg["tq"], cfg["head_group"])
    else:
        cache_k, cache_v, page_table = cache
        hv = cache_k.shape[2:]
        o = _attn_decode(qq.reshape((b, s) + hv), k_sh.reshape((b, s) + hv), v_sh.reshape((b, s) + hv),
                         cache_k, cache_v, page_table, wts["b_lambda"], wts["slopes"], lam_init).reshape(b, s, d)
    heads_b = wts["slopes"].shape[0]
    x3, h2p, ri = _mixer_out(
        functools.partial(_mixb_body, heads=heads_b, gain=1.0 - lam_init), [rs(o)], x2, m1,
        [wts["b_subln_w"], wts["b_w_out"], wts["ln_g"][1][0], wts["ln_b"][1][0], *wts["router"][1]],
        cfg["ts_out"], cfg["packed"], "mix_b")
    y = moe(h2p, ri, x3, m1, 1)
    return (y.reshape(b, s, d), k_sh.reshape(b, s, d), v_sh.reshape(b, s, d), conv_new, c_new, n_new, m_new)


def kernel(x_prompt, x_sample, cache_k, cache_v, state_conv, state_C, state_n, state_m, page_table, c_prompt, c_sample, a_w_in, a_conv_w, a_conv_b, a_w_q, a_w_k, a_w_v, a_w_gate, a_b_gate, a_norm_w, a_skip, a_w_out, kv_w_k, kv_w_v, b_w_q, b_lambda, b_subln_w, b_w_out, moe_w_group, moe_b_group, moe_w_expert, moe_b_expert, moe_w1, moe_w3, moe_w2, ln_g, ln_b, ada_w, ada_b):
    bp, sp, d = x_prompt.shape
    db, ds, _ = x_sample.shape
    assert ds == 1
    heads_a = state_C.shape[2]
    heads_b = cache_k.shape[2]
    inner = a_conv_w.shape[2]
    hd_b = d // (2 * heads_b)
    ff = moe_w1.shape[-1]

    slopes = jnp.exp2(-8.0 * jnp.arange(1, heads_b + 1, dtype=F32) / heads_b)
    common = dict(
        heads_a=heads_a,
        a_norm_w=a_norm_w[0].reshape(1, inner), a_skip=a_skip[0].reshape(1, inner),
        q_scale=hd_b ** -0.5, b_lambda=b_lambda[0], b_subln_w=b_subln_w[0].reshape(1, -1), slopes=slopes,
        router=[_prep_router(moe_w_group[i], moe_b_group[i], moe_w_expert[i], moe_b_expert[i]) for i in range(DEPTH)],
        moe=[(moe_w1[i].reshape(N_EXPERTS, d, ff), moe_w3[i].reshape(N_EXPERTS, d, ff),
              moe_w2[i].reshape(N_EXPERTS, ff, d)) for i in range(DEPTH)],
        ln_g=[[ln_g[i, j].reshape(1, d) for j in range(2)] for i in range(DEPTH)],
        ln_b=[[ln_b[i, j].reshape(1, d) for j in range(2)] for i in range(DEPTH)],
    )

    def weights(dt):
        return dict(
            common,
            a_w_in=a_w_in[0].astype(dt),
            a_prep=_prep_mlstm(a_conv_w[0], a_conv_b[0], a_w_q[0], a_w_k[0], a_w_v[0], a_w_gate[0], a_b_gate[0],
                               heads_a, dt),
            a_w_out=a_w_out[0].astype(dt),
            kv_w=jnp.concatenate([kv_w_k, kv_w_v], axis=1).astype(dt),
            b_w_q=b_w_q[0].astype(dt), b_w_out=b_w_out[0].astype(dt))

    mod = _ada(jnp.concatenate([c_prompt, c_sample], axis=0), ada_w, ada_b)
    mod_p = mod[:, :bp].reshape(DEPTH, bp, 1, 6 * d)
    mod_s = mod[:, bp:].reshape(DEPTH, 1, db, 6 * d)

    cfg_p = dict(ts=512, ts_pre=256, ts_out=256, chunk=256, row_shape=(bp, sp),
                 nbt=sp, tmx=256, tc=512, tq=512,
                 head_group=4, tn=None, packed=False, dense_moe=False, q_dtype=BF16)
    zero_conv = jnp.zeros((bp, A_CONV - 1, inner), F32)
    outs_p = _trunk(x_prompt, mod_p, zero_conv, None, None, weights(BF16), cfg_p)

    cfg_s = dict(ts=db, ts_pre=1, ts_out=db, chunk=None, row_shape=(1, db), tn=d, packed=False, dense_moe=True,
                 q_dtype=F32)
    cache = (cache_k, cache_v, page_table)
    outs_s = _trunk(x_sample, mod_s, state_conv[0], (state_C[0], state_n[0], state_m[0]), cache, weights(F32), cfg_s)

    def pack(o, nb_, s_):
        y, k, v, conv, c, n, m = o
        return (y, k.reshape(nb_, s_, heads_b, 2 * hd_b), v.reshape(nb_, s_, heads_b, 2 * hd_b),
                conv[None], c[None], n[None], m[None])

    yp, kp, vp, convp, cp, np_, mp = pack(outs_p, bp, sp)
    ys, ks, vs, convs, cs, ns, ms_ = pack(outs_s, db, ds)
    return (yp, ys, kp, vp, convp, cp, np_, mp, ks, vs, convs, cs, ns, ms_)
```

```python
import functools
import math

import jax
import jax.numpy as jnp
from jax import lax
from jax.experimental import pallas as pl
from jax.experimental.pallas import tpu as pltpu

F32 = jnp.float32
BF16 = jnp.bfloat16
I32 = jnp.int32
U32 = jnp.uint32

DEPTH = 2
LN_EPS = 1e-5
DN_ALPHA = (2.0 * DEPTH) ** 0.25
A_CONV = 4
A_QKV_BLOCK = 4
MOE_GROUPS = 4
MOE_EXPERTS = 4
N_EXPERTS = MOE_GROUPS * MOE_EXPERTS
PAGE_SIZE = 128
LANES = 128
PAGES_PER_STEP = 16
MIB = 1024 * 1024


def _cparams(sem, vmem_mib=48):
    return pltpu.CompilerParams(dimension_semantics=sem, vmem_limit_bytes=vmem_mib * MIB)


def _silu(x):
    return x / (1.0 + jnp.exp(-x))


def _log_sigmoid(x):
    return jnp.minimum(x, 0.0) - jnp.log(1.0 + jnp.exp(-jnp.abs(x)))


def _dot(a, b):
    return jnp.dot(a, b, preferred_element_type=F32)


def _dot_nt(a, b):
    return lax.dot_general(a, b, (((1,), (1,)), ((), ())), preferred_element_type=F32)


def _dot_tn(a, b):
    return lax.dot_general(a, b, (((0,), (0,)), ((), ())), preferred_element_type=F32)


def _split(x):
    hi = x.astype(BF16)
    return hi, (x - hi.astype(F32)).astype(BF16)


def _dot3(a, w, dot=None):
    dot = dot or _dot
    ah, al = _split(a)
    wh, wl = _split(w)
    return dot(ah, wh) + dot(al, wh) + dot(ah, wl)


def _mm(a, w):
    if w.dtype == F32:
        return _dot3(a.astype(F32), w)
    return _dot(a.astype(BF16), w)


def _bits(x):
    return lax.bitcast_convert_type(x, U32)


def _pack_pairs(y):
    w = y.shape[-1] // 2
    r = _bits(y.astype(BF16).astype(F32))
    return (r[:, w:] & jnp.uint32(0xFFFF0000)) | (r[:, :w] >> 16)


def _unpack_pairs(u):
    lo = lax.bitcast_convert_type(u << 16, F32)
    hi = lax.bitcast_convert_type(u & jnp.uint32(0xFFFF0000), F32)
    return lo, hi


def _layer_norm(y, g, b):
    mu = jnp.mean(y, axis=-1, keepdims=True)
    yc = y - mu
    var = jnp.mean(yc * yc, axis=-1, keepdims=True)
    return yc * lax.rsqrt(var + LN_EPS) * g + b


def _ada_body(c_ref, w_ref, b_ref, o_ref):
    s = _silu(c_ref[...])
    o_ref[0] = _dot3(s, w_ref[0]) + b_ref[0]


def _ada(c_all, ada_w, ada_b):
    r, d = c_all.shape
    depth, _, n = ada_w.shape
    tn = 1536
    return pl.pallas_call(
        _ada_body,
        grid=(depth, n // tn),
        in_specs=[pl.BlockSpec((r, d), lambda i, j: (0, 0)),
                  pl.BlockSpec((1, d, tn), lambda i, j: (i, 0, j)),
                  pl.BlockSpec((1, 1, tn), lambda i, j: (i, 0, j))],
        out_specs=pl.BlockSpec((1, r, tn), lambda i, j: (i, 0, j)),
        out_shape=jax.ShapeDtypeStruct((depth, r, n), F32),
        compiler_params=_cparams(("parallel", "parallel")),
        name="ada",
    )(c_all, ada_w, ada_b.reshape(depth, 1, n))


def _mod_spec(mod, ts, col):
    d = mod.shape[-1] // 6
    if mod.shape[1] == 1:
        return pl.BlockSpec((1, 1, d), lambda b, s, *_: (b, 0, col))
    return pl.BlockSpec((1, ts, d), lambda b, s, *_: (b, s, col))


def _lin_body(*refs, has_mod, out_scale):
    if has_mod:
        x_ref, sc_ref, sh_ref, w_ref, *o_refs = refs
        x = x_ref[0] * (1.0 + sc_ref[0]) + sh_ref[0]
    else:
        x_ref, w_ref, *o_refs = refs
        x = x_ref[0]
    y = _mm(x, w_ref[...])
    if out_scale != 1.0:
        y = y * out_scale
    n = y.shape[-1] // len(o_refs)
    for j, o_ref in enumerate(o_refs):
        o_ref[0] = y[:, j * n:(j + 1) * n].astype(o_ref.dtype)


def _linear(x, w, ts, out_dtypes, mod=None, cols=None, out_scale=1.0, tn=None, name="linear"):
    b, s, k = x.shape
    n = w.shape[1]
    no = n // len(out_dtypes)
    tn = tn or n
    assert n % tn == 0 and (tn == n or no % tn == 0)
    in_specs = [pl.BlockSpec((1, ts, k), lambda i, j, c: (i, j, 0))]
    args = [x]
    if mod is not None:
        in_specs += [_mod_spec(mod, ts, cols[0]), _mod_spec(mod, ts, cols[1])]
        args += [mod, mod]
    in_specs.append(pl.BlockSpec((k, tn), lambda i, j, c: (0, c)))
    args.append(w)
    if tn == n:
        out_specs = [pl.BlockSpec((1, ts, no), lambda i, j, c: (i, j, 0)) for _ in out_dtypes]
        out_shape = [jax.ShapeDtypeStruct((b, s, no), dt) for dt in out_dtypes]
    else:
        assert len(set(out_dtypes)) == 1
        out_specs = [pl.BlockSpec((1, ts, tn), lambda i, j, c: (i, j, c))]
        out_shape = [jax.ShapeDtypeStruct((b, s, n), out_dtypes[0])]
    outs = pl.pallas_call(
        functools.partial(_lin_body, has_mod=mod is not None, out_scale=out_scale),
        grid=(b, s // ts, n // tn),
        in_specs=in_specs,
        out_specs=out_specs,
        out_shape=out_shape,
        compiler_params=_cparams(("parallel", "parallel", "parallel"), 56),
        name=name,
    )(*args)
    if tn == n:
        return outs
    return [outs[0][:, :, g * no:(g + 1) * no] for g in range(len(out_dtypes))]


def _p1_body(xm_ref, tail_ref, cw_ref, cb_ref, wqkv_ref, wg_ref, wgt_ref, bg_ref, bgt_ref,
             xc_ref, q_ref, k_ref, v_ref, g_ref, conv_ref, xbuf):
    s = pl.program_id(1)
    ts = xm_ref.shape[1]

    @pl.when(s == 0)
    def _():
        xbuf[0:8, :] = tail_ref[0]

    @pl.when(s > 0)
    def _():
        xbuf[0:8, :] = xbuf[ts:ts + 8, :]

    xm = xm_ref[0]
    xbuf[8:8 + ts, :] = xm
    conv = cb_ref[...] + xm * cw_ref[A_CONV - 1:A_CONV, :]
    for j in range(A_CONV - 1):
        conv = conv + xbuf[5 + j:5 + j + ts, :] * cw_ref[j:j + 1, :]
    conv_ref[0] = xbuf[ts:ts + 8, :]
    _p1_tail(_silu(conv), xm, wqkv_ref, wg_ref, wgt_ref, bg_ref, bgt_ref, xc_ref, q_ref, k_ref, v_ref, g_ref, True)


def _p1_rows_body(xm_ref, hist_ref, cw_ref, cb_ref, wqkv_ref, wg_ref, wgt_ref, bg_ref, bgt_ref,
                  xc_ref, q_ref, k_ref, v_ref, g_ref):
    xm = xm_ref[0]
    conv = cb_ref[...] + xm * cw_ref[A_CONV - 1:A_CONV, :]
    for j in range(A_CONV - 1):
        conv = conv + hist_ref[j] * cw_ref[j:j + 1, :]
    _p1_tail(_silu(conv), xm, wqkv_ref, wg_ref, wgt_ref, bg_ref, bgt_ref, xc_ref, q_ref, k_ref, v_ref, g_ref, False)


def _p1_tail(xc, xm, wqkv_ref, wg_ref, wgt_ref, bg_ref, bgt_ref, xc_ref, q_ref, k_ref, v_ref, g_ref, gates_on_lanes):
    inner = xm.shape[1]
    xc_ref[0] = xc
    cw = wqkv_ref.shape[2]
    for c in range(inner // cw):
        sl = slice(c * cw, (c + 1) * cw)
        q_ref[0, :, sl] = _mm(xc[:, sl], wqkv_ref[0, c]).astype(q_ref.dtype)
        k_ref[0, :, sl] = _mm(xc[:, sl], wqkv_ref[1, c]).astype(k_ref.dtype)
        v_ref[0, :, sl] = _mm(xm[:, sl], wqkv_ref[2, c]).astype(v_ref.dtype)
    q = q_ref[0]
    k = k_ref[0]
    v = v_ref[0]
    half = wgt_ref.shape[0] // 2
    if gates_on_lanes:
        gt = (_dot_nt(wgt_ref[:, 0:inner], q) + _dot_nt(wgt_ref[:, inner:2 * inner], k)
              + _dot_nt(wgt_ref[:, 2 * inner:3 * inner], v) + bgt_ref[...])
        sub = lax.broadcasted_iota(I32, gt.shape, 0)
        g_ref[0] = jnp.where(sub >= half, _log_sigmoid(gt), gt)
    else:
        g = (_mm(q, wg_ref[0:inner, :]) + _mm(k, wg_ref[inner:2 * inner, :])
             + _mm(v, wg_ref[2 * inner:3 * inner, :]) + bg_ref[...])
        lane = lax.broadcasted_iota(I32, g.shape, 1)
        g_ref[0] = jnp.where(lane >= half, _log_sigmoid(g), g)[:, 0:8]


def _mlstm_pre(xm, tail8, prep, ts):
    b, s, inner = xm.shape
    assert ts >= 8 and s % ts == 0
    full = lambda shape: pl.BlockSpec(shape, lambda i, j: (0,) * len(shape))
    row = lambda n, dt: (pl.BlockSpec((1, ts, n), lambda i, j: (i, j, 0)), jax.ShapeDtypeStruct((b, s, n), dt))
    qdt = BF16 if prep["wqkv"].dtype == BF16 else F32
    outs = [row(inner, F32), row(inner, qdt), row(inner, qdt), row(inner, qdt),
            (pl.BlockSpec((1, 8, ts), lambda i, j: (i, 0, j)), jax.ShapeDtypeStruct((b, 8, s), F32)),
            (pl.BlockSpec((1, 8, inner), lambda i, j: (i, 0, 0)), jax.ShapeDtypeStruct((b, 8, inner), F32))]
    return pl.pallas_call(
        _p1_body,
        grid=(b, s // ts),
        in_specs=[pl.BlockSpec((1, ts, inner), lambda i, j: (i, j, 0)),
                  pl.BlockSpec((1, 8, inner), lambda i, j: (i, 0, 0)),
                  full((A_CONV, inner)), full((1, inner)),
                  full(prep["wqkv"].shape),
                  full((3 * inner, LANES)), full((8, 3 * inner)), full((1, LANES)), full((8, 1))],
        out_specs=[o[0] for o in outs],
        out_shape=[o[1] for o in outs],
        scratch_shapes=[pltpu.VMEM((ts + 8, inner), F32)],
        compiler_params=_cparams(("parallel", "arbitrary"), 56),
        name="mlstm_pre",
    )(xm, tail8, prep["conv_w"], prep["conv_b"], prep["wqkv"],
      prep["wg"], prep["wgt"], prep["bg"], prep["bgt"])


def _mlstm_pre_rows(xm, hist, prep):
    _, r, inner = xm.shape
    full = lambda shape: pl.BlockSpec(shape, lambda i: (0,) * len(shape))
    row = lambda n, dt: (pl.BlockSpec((1, r, n), lambda i: (0, 0, 0)), jax.ShapeDtypeStruct((1, r, n), dt))
    qdt = BF16 if prep["wqkv"].dtype == BF16 else F32
    outs = [row(inner, F32), row(inner, qdt), row(inner, qdt), row(inner, qdt), row(8, F32)]
    return pl.pallas_call(
        _p1_rows_body,
        grid=(1,),
        in_specs=[full((1, r, inner)), full((A_CONV - 1, r, inner)),
                  full((A_CONV, inner)), full((1, inner)),
                  full(prep["wqkv"].shape),
                  full((3 * inner, LANES)), full((8, 3 * inner)), full((1, LANES)), full((8, 1))],
        out_specs=[o[0] for o in outs],
        out_shape=[o[1] for o in outs],
        compiler_params=_cparams(("arbitrary",), 56),
        name="mlstm_pre_rows",
    )(xm, hist, prep["conv_w"], prep["conv_b"], prep["wqkv"],
      prep["wg"], prep["wgt"], prep["bg"], prep["bgt"])


def _p2_body(q_ref, k_ref, v_ref, gr_ref, h_ref, c_ref, n_ref, m_ref, caug, ms):
    c = pl.program_id(1)
    nc = pl.num_programs(1)
    L = q_ref.shape[1]
    heads = gr_ref.shape[1]
    hd = q_ref.shape[2] // heads

    @pl.when(c == 0)
    def _():
        caug[...] = jnp.zeros_like(caug)
        ms[...] = jnp.zeros_like(ms)

    row = lax.broadcasted_iota(I32, (L, L), 0)
    col = lax.broadcasted_iota(I32, (L, L), 1)
    causal = col <= row
    diag = col == row
    lane = lax.broadcasted_iota(I32, (L, LANES), 1)
    ones_col = jnp.where(lane == 0, 1.0, 0.0).astype(BF16)
    for a in range(heads):
        sl = slice(a * hd, (a + 1) * hd)
        q = q_ref[0, :, sl]
        k = k_ref[0, :, sl]
        v = v_ref[0, :, sl]
        gr = gr_ref[0, a]
        ig_r, lf_r = gr[0:1, :], gr[1:2, :]
        ig_c = jnp.sum(jnp.where(diag, ig_r, 0.0), axis=1, keepdims=True)
        lf_c = jnp.sum(jnp.where(diag, lf_r, 0.0), axis=1, keepdims=True)
        b_c = jnp.sum(jnp.where(causal, lf_r, 0.0), axis=1, keepdims=True)
        b_r = jnp.sum(jnp.where(row <= col, lf_c, 0.0), axis=0, keepdims=True)
        log_d = jnp.where(causal, b_c - b_r + ig_r, -jnp.inf)
        m_prev = ms[a]
        log_inter = b_c + m_prev
        m_t = jnp.maximum(log_inter, jnp.max(log_d, axis=1, keepdims=True))
        d = jnp.exp(log_d - m_t)
        w_inter = jnp.exp(log_inter - m_t)
        s = (_dot_nt(q, k) * d).astype(BF16)
        vaug = jnp.concatenate([v, ones_col], axis=1)
        ca = caug[a]
        num = w_inter * _dot(q, ca.astype(BF16)) + _dot(s, vaug)
        den = num[:, hd:hd + 1]
        h_ref[0, :, sl] = num[:, 0:hd] / jnp.maximum(jnp.abs(den), jnp.exp(-m_t))
        m_new = m_t[L - 1:L, :]
        b_last = b_c[L - 1:L, :]
        w_s = jnp.exp(b_last - b_c + ig_c - m_new)
        decay = jnp.exp(b_last + m_prev - m_new)
        kw = (k.astype(F32) * w_s).astype(BF16)
        caug[a] = decay * ca + _dot_tn(kw, vaug)
        ms[a] = m_new

    @pl.when(c == nc - 1)
    def _():
        for a in range(heads):
            c_ref[0, a] = caug[a, :, 0:hd]
            n_ref[0, a] = caug[a, :, hd:hd + 1]
            m_ref[0, a] = ms[a]


def _mlstm_scan(q, k, v, grow, L):
    b, s, inner = q.shape
    heads = grow.shape[1]
    hd = inner // heads
    qkv = pl.BlockSpec((1, L, inner), lambda i, c: (i, c, 0))
    return pl.pallas_call(
        _p2_body,
        grid=(b, s // L),
        in_specs=[qkv, qkv, qkv,
                  pl.BlockSpec((1, heads, 2, L), lambda i, c: (i, 0, 0, c))],
        out_specs=[qkv,
                   pl.BlockSpec((1, heads, hd, hd), lambda i, c: (i, 0, 0, 0)),
                   pl.BlockSpec((1, heads, hd, 1), lambda i, c: (i, 0, 0, 0)),
                   pl.BlockSpec((1, heads, 1, 1), lambda i, c: (i, 0, 0, 0))],
        out_shape=[jax.ShapeDtypeStruct((b, s, inner), F32),
                   jax.ShapeDtypeStruct((b, heads, hd, hd), F32),
                   jax.ShapeDtypeStruct((b, heads, hd, 1), F32),
                   jax.ShapeDtypeStruct((b, heads, 1, 1), F32)],
        scratch_shapes=[pltpu.VMEM((heads, hd, hd + LANES), F32), pltpu.VMEM((heads, 1, 1), F32)],
        compiler_params=_cparams(("parallel", "arbitrary")),
        name="mlstm_scan",
    )(q, k, v, grow)


def _s2_body(q_ref, k_ref, v_ref, g_ref, c_ref, n_ref, m_ref, h_ref, co_ref, no_ref, mo_ref):
    heads = c_ref.shape[1]
    hd = c_ref.shape[2]
    sub = lax.broadcasted_iota(I32, (8, hd), 0)
    first = sub == 0
    for a in range(heads):
        sl = slice(a * hd, (a + 1) * hd)
        q = q_ref[0, :, sl]
        k = k_ref[0, :, sl]
        v = v_ref[0, :, sl]
        g = g_ref[0, a]
        ig, lf = g[:, 0:1], g[:, 1:2]
        m_prev = m_ref[0, a]
        m_new = jnp.maximum(lf + m_prev, ig)
        decay = jnp.exp(lf + m_prev - m_new)
        dd = jnp.exp(ig - m_new)
        k8 = jnp.where(first, k, 0.0)
        v8 = jnp.where(first, v, 0.0)
        q8 = jnp.where(first, q, 0.0)
        c_new = decay * c_ref[0, a] + dd * _dot3(k8, v8, _dot_tn)
        n_new = decay * n_ref[0, a] + dd * k
        num = _dot3(q8, c_new)[0:1, :]
        den = jnp.sum(q * n_new, axis=1, keepdims=True)
        h_ref[0, :, sl] = num / jnp.maximum(jnp.abs(den), jnp.exp(-m_new))
        co_ref[0, a] = c_new
        no_ref[0, a] = n_new
        mo_ref[0, a] = m_new


def _mlstm_step(q, k, v, g4, c0, n0, m0):
    b, _, inner = q.shape
    heads = c0.shape[1]
    hd = inner // heads
    qkv = pl.BlockSpec((1, 1, inner), lambda i: (i, 0, 0))
    st = lambda r, c: pl.BlockSpec((1, heads, r, c), lambda i: (i, 0, 0, 0))
    return pl.pallas_call(
        _s2_body,
        grid=(b,),
        in_specs=[qkv, qkv, qkv, st(1, 2), st(hd, hd), st(1, hd), st(1, 1)],
        out_specs=[qkv, st(hd, hd), st(1, hd), st(1, 1)],
        out_shape=[jax.ShapeDtypeStruct((b, 1, inner), F32),
                   jax.ShapeDtypeStruct((b, heads, hd, hd), F32),
                   jax.ShapeDtypeStruct((b, heads, 1, hd), F32),
                   jax.ShapeDtypeStruct((b, heads, 1, 1), F32)],
        compiler_params=_cparams(("parallel",)),
        name="mlstm_step",
    )(q, k, v, g4, c0, n0, m0)


def _route(logits):
    ts = logits.shape[0]
    lane = lax.broadcasted_iota(I32, (ts, LANES), 1)
    neg = -jnp.inf
    big = jnp.int32(1 << 20)
    is_g = lane < MOE_GROUPS
    gl = jnp.where(is_g, logits, neg)
    gmax = jnp.max(gl, axis=1, keepdims=True)
    gidx = jnp.min(jnp.where(gl == gmax, lane, big), axis=1, keepdims=True)
    gsum = jnp.sum(jnp.where(is_g, jnp.exp(logits - gmax), 0.0), axis=1, keepdims=True)
    g_w = 1.0 / gsum
    lo = MOE_GROUPS + MOE_EXPERTS * gidx
    el = jnp.where(lane >= lo, jnp.where(lane < lo + MOE_EXPERTS, logits, neg), neg)
    v1 = jnp.max(el, axis=1, keepdims=True)
    i1 = jnp.min(jnp.where(el == v1, lane, big), axis=1, keepdims=True)
    el2 = jnp.where(lane == i1, neg, el)
    v2 = jnp.max(el2, axis=1, keepdims=True)
    i2 = jnp.min(jnp.where(el2 == v2, lane, big), axis=1, keepdims=True)
    t = jnp.exp(v2 - v1)
    w1 = g_w / (1.0 + t)
    w2 = g_w * t / (1.0 + t)
    e1 = (i1 - MOE_GROUPS).astype(F32)
    e2 = (i2 - MOE_GROUPS).astype(F32)
    return jnp.where(lane == 0, e1, jnp.where(lane == 1, e2, jnp.where(lane == 2, w1, jnp.where(lane == 3, w2, 0.0))))


def _epilogue(x, o, g1, sc2, sh2, lng, lnb, wrh, wrl, br, x1_ref, h2_ref, ri_ref):
    x1 = _layer_norm(DN_ALPHA * x + (1.0 + g1) * o, lng, lnb)
    x1_ref[0] = x1
    h2 = x1 * (1.0 + sc2) + sh2
    h2_ref[0] = _pack_pairs(h2) if h2_ref.dtype == U32 else h2
    xh = h2.astype(BF16)
    xl = (h2 - xh.astype(F32)).astype(BF16)
    logits = _dot(xh, wrh) + _dot(xl, wrh) + _dot(xh, wrl) + br
    ri_ref[0] = _route(logits)


def _mixa_body(h_ref, xc_ref, z_ref, x_ref, g1_ref, sc2_ref, sh2_ref, nw_ref, skip_ref, wo_ref,
               lng_ref, lnb_ref, wrh_ref, wrl_ref, br_ref, x1_ref, h2_ref, ri_ref, *, heads):
    h = h_ref[0]
    inner = h.shape[1]
    hd = inner // heads
    parts = []
    for a in range(heads):
        ha = h[:, a * hd:(a + 1) * hd]
        hc = ha - jnp.mean(ha, axis=1, keepdims=True)
        parts.append(hc * lax.rsqrt(jnp.mean(hc * hc, axis=1, keepdims=True) + LN_EPS))
    hn = jnp.concatenate(parts, axis=1) * nw_ref[...]
    z = z_ref[0]
    out = (hn + skip_ref[...] * xc_ref[0]) * (1.0 / (1.0 + jnp.exp(-z)))
    o = _mm(out, wo_ref[...])
    _epilogue(x_ref[0], o, g1_ref[0], sc2_ref[0], sh2_ref[0], lng_ref[...], lnb_ref[...],
              wrh_ref[...], wrl_ref[...], br_ref[...], x1_ref, h2_ref, ri_ref)


def _mixb_body(o_ref, x_ref, g1_ref, sc2_ref, sh2_ref, sw_ref, wo_ref,
               lng_ref, lnb_ref, wrh_ref, wrl_ref, br_ref, x1_ref, h2_ref, ri_ref, *, heads, gain):
    o = o_ref[0]
    vd = o.shape[1] // heads
    parts = []
    for a in range(heads):
        oa = o[:, a * vd:(a + 1) * vd]
        parts.append(oa * lax.rsqrt(jnp.mean(oa * oa, axis=1, keepdims=True) + LN_EPS) * sw_ref[...] * gain)
    on = jnp.concatenate(parts, axis=1)
    y = _mm(on, wo_ref[...])
    _epilogue(x_ref[0], y, g1_ref[0], sc2_ref[0], sh2_ref[0], lng_ref[...], lnb_ref[...],
              wrh_ref[...], wrl_ref[...], br_ref[...], x1_ref, h2_ref, ri_ref)


def _mixer_out(body, acts, x, mod, consts, ts, packed, name):
    b, s, d = x.shape
    hw, hdt = (d // 2, U32) if packed else (d, F32)
    rowspec = lambda a: pl.BlockSpec((1, ts, a.shape[2]), lambda i, j: (i, j, 0))
    full = lambda a: pl.BlockSpec(a.shape, lambda i, j: (0,) * a.ndim)
    in_specs = [rowspec(a) for a in acts] + [rowspec(x)]
    in_specs += [_mod_spec(mod, ts, 2), _mod_spec(mod, ts, 4), _mod_spec(mod, ts, 3)]
    in_specs += [full(c) for c in consts]
    return pl.pallas_call(
        body,
        grid=(b, s // ts),
        in_specs=in_specs,
        out_specs=[pl.BlockSpec((1, ts, d), lambda i, j: (i, j, 0)),
                   pl.BlockSpec((1, ts, hw), lambda i, j: (i, j, 0)),
                   pl.BlockSpec((1, ts, LANES), lambda i, j: (i, j, 0))],
        out_shape=[jax.ShapeDtypeStruct((b, s, d), F32),
                   jax.ShapeDtypeStruct((b, s, hw), hdt),
                   jax.ShapeDtypeStruct((b, s, LANES), F32)],
        compiler_params=_cparams(("parallel", "parallel"), 56),
        name=name,
    )(*acts, x, mod, mod, mod, *consts)


def _sort_body(e1_ref, e2_ref, pa_ref, pb_ref, cnt_ref, *, tmx, cw):
    nb_tok = e1_ref.shape[2]
    e1 = e1_ref[0]
    e2 = e2_ref[0]
    sub = lax.broadcasted_iota(I32, (N_EXPERTS, nb_tok), 0)
    a_hot = sub == e1
    b_hot = sub == e2
    m = jnp.where(a_hot, 1.0, 0.0) + jnp.where(b_hot, 1.0, 0.0)
    cnt = jnp.sum(m, axis=1, keepdims=True)
    padded = jnp.floor((cnt + (tmx - 1)) / tmx) * tmx
    r16 = lax.broadcasted_iota(I32, (N_EXPERTS, N_EXPERTS), 0)
    c16 = lax.broadcasted_iota(I32, (N_EXPERTS, N_EXPERTS), 1)
    prow = jnp.sum(jnp.where(r16 == c16, padded, 0.0), axis=0, keepdims=True)
    segoff = jnp.sum(jnp.where(c16 < r16, prow, 0.0), axis=1, keepdims=True)
    cnt_ref[0] = jnp.broadcast_to(cnt, (N_EXPERTS, LANES)).astype(I32)
    ur = lax.broadcasted_iota(I32, (cw, cw), 0)
    uc = lax.broadcasted_iota(I32, (cw, cw), 1)
    upper = jnp.where(ur < uc, 1.0, 0.0).astype(BF16)
    carry = segoff
    for j in range(nb_tok // cw):
        sl = slice(j * cw, (j + 1) * cw)
        mc = m[:, sl]
        rank = _dot(mc.astype(BF16), upper) + carry
        pa_ref[0, :, sl] = jnp.sum(jnp.where(a_hot[:, sl], rank, 0.0), axis=0, keepdims=True).astype(I32)
        pb_ref[0, :, sl] = jnp.sum(jnp.where(b_hot[:, sl], rank, 0.0), axis=0, keepdims=True).astype(I32)
        carry = carry + jnp.sum(mc, axis=1, keepdims=True)


def _moe_sort(e1, e2, tmx):
    nb, _, nbt = e1.shape
    cw = min(nbt, 256)
    tok = pl.BlockSpec((1, 1, nbt), lambda i: (i, 0, 0))
    return pl.pallas_call(
        functools.partial(_sort_body, tmx=tmx, cw=cw),
        grid=(nb,),
        in_specs=[tok, tok],
        out_specs=[tok, tok, pl.BlockSpec((1, N_EXPERTS, LANES), lambda i: (i, 0, 0))],
        out_shape=[jax.ShapeDtypeStruct((nb, 1, nbt), I32), jax.ShapeDtypeStruct((nb, 1, nbt), I32),
                   jax.ShapeDtypeStruct((nb, N_EXPERTS, LANES), I32)],
        compiler_params=_cparams(("parallel",)),
        name="moe_sort",
    )(e1, e2)


def _dispatch_body(pa_ref, pb_ref, src_ref, dst_ref):
    nbt = src_ref.shape[1]
    dst_ref[...] = jnp.zeros_like(dst_ref)

    def body(i, carry):
        row = src_ref[0, pl.ds(i, 1), :]
        dst_ref[0, pl.ds(pa_ref[0, 0, i], 1), :] = row
        dst_ref[0, pl.ds(pb_ref[0, 0, i], 1), :] = row
        return carry

    lax.fori_loop(0, nbt, body, 0, unroll=min(8, nbt))


def _moe_dispatch(pa, pb, h2p, rp):
    nb, nbt, w = h2p.shape
    smem = pl.BlockSpec((1, 1, nbt), lambda i: (i, 0, 0), memory_space=pltpu.SMEM)
    return pl.pallas_call(
        _dispatch_body,
        grid=(nb,),
        in_specs=[smem, smem, pl.BlockSpec((1, nbt, w), lambda i: (i, 0, 0))],
        out_specs=pl.BlockSpec((1, rp, w), lambda i: (i, 0, 0), pipeline_mode=pl.Buffered(1)),
        out_shape=jax.ShapeDtypeStruct((nb, rp, w), h2p.dtype),
        compiler_params=_cparams(("arbitrary",), 56),
        name="moe_dispatch",
    )(pa, pb, h2p)


def _expert_body(tb_ref, tr_ref, te_ref, tv_ref, tf_ref, x_ref, w1_ref, w3_ref, w2_ref, o_ref, w1b, w3b, w2b):
    t = pl.program_id(0)

    @pl.when(tf_ref[t] == 1)
    def _():
        w1b[...] = w1_ref[0].astype(BF16)
        w3b[...] = w3_ref[0].astype(BF16)
        w2b[...] = w2_ref[0].astype(BF16)

    @pl.when(tv_ref[t] == 1)
    def _():
        if x_ref.dtype == U32:
            lo, hi = _unpack_pairs(x_ref[0])
            lo = lo.astype(BF16)
            hi = hi.astype(BF16)
            half = lo.shape[1]
            a = _dot(lo, w1b[0:half, :]) + _dot(hi, w1b[half:2 * half, :])
            g = _dot(lo, w3b[0:half, :]) + _dot(hi, w3b[half:2 * half, :])
        else:
            xb = x_ref[0].astype(BF16)
            a = _dot(xb, w1b[...])
            g = _dot(xb, w3b[...])
        y = _dot((_silu(a) * g).astype(BF16), w2b[...])
        o_ref[0] = _pack_pairs(y) if o_ref.dtype == U32 else y


def _moe_experts(tables, xs, w1, w3, w2, tmx):
    nb, rp, half = xs.shape
    ne, d, ff = w1.shape
    nt = tables[0].shape[0]
    xspec = pl.BlockSpec((1, tmx, half), lambda t, tb, tr, te, tv, tf: (tb[t], tr[t], 0))
    grid_spec = pltpu.PrefetchScalarGridSpec(
        num_scalar_prefetch=5,
        grid=(nt,),
        in_specs=[xspec,
                  pl.BlockSpec((1, d, ff), lambda t, tb, tr, te, tv, tf: (te[t], 0, 0)),
                  pl.BlockSpec((1, d, ff), lambda t, tb, tr, te, tv, tf: (te[t], 0, 0)),
                  pl.BlockSpec((1, ff, d), lambda t, tb, tr, te, tv, tf: (te[t], 0, 0))],
        out_specs=xspec,
        scratch_shapes=[pltpu.VMEM((d, ff), BF16), pltpu.VMEM((d, ff), BF16), pltpu.VMEM((ff, d), BF16)],
    )
    return pl.pallas_call(
        _expert_body,
        grid_spec=grid_spec,
        out_shape=jax.ShapeDtypeStruct((nb, rp, half), xs.dtype),
        input_output_aliases={len(tables): 0},
        compiler_params=_cparams(("arbitrary",)),
        name="moe_experts",
    )(*tables, xs, w1, w3, w2)


def _combine_body(pa_ref, pb_ref, ri_ref, ys_ref, x1_ref, g2_ref, lng_ref, lnb_ref, o_ref, ga, gb):
    tc = x1_ref.shape[1]

    def body(i, carry):
        ga[pl.ds(i, 1), :] = ys_ref[0, pl.ds(pa_ref[0, 0, i], 1), :]
        gb[pl.ds(i, 1), :] = ys_ref[0, pl.ds(pb_ref[0, 0, i], 1), :]
        return carry

    lax.fori_loop(0, tc, body, 0, unroll=min(8, tc))
    ri = ri_ref[0]
    wa = ri[:, 2:3]
    wb = ri[:, 3:4]
    if ga.dtype == U32:
        alo, ahi = _unpack_pairs(ga[...])
        blo, bhi = _unpack_pairs(gb[...])
        ffn = jnp.concatenate([wa * alo + wb * blo, wa * ahi + wb * bhi], axis=1)
    else:
        ffn = wa * ga[...] + wb * gb[...]
    y = DN_ALPHA * x1_ref[0] + (1.0 + g2_ref[0]) * ffn
    o_ref[0] = _layer_norm(y, lng_ref[...], lnb_ref[...])


def _moe_combine(pa, pb, ri, ys, x1, mod, lng, lnb, tc):
    nb, nbt, d = x1.shape
    rp, half = ys.shape[1:]
    smem = pl.BlockSpec((1, 1, tc), lambda i, j: (i, 0, j), memory_space=pltpu.SMEM)
    per_mod = (nb * nbt) // mod.shape[0]
    assert mod.shape[1] == 1 and per_mod % tc == 0
    return pl.pallas_call(
        _combine_body,
        grid=(nb, nbt // tc),
        in_specs=[smem, smem,
                  pl.BlockSpec((1, tc, LANES), lambda i, j: (i, j, 0)),
                  pl.BlockSpec((1, rp, half), lambda i, j: (i, 0, 0), pipeline_mode=pl.Buffered(1)),
                  pl.BlockSpec((1, tc, d), lambda i, j: (i, j, 0)),
                  pl.BlockSpec((1, 1, d), lambda i, j: ((i * nbt + j * tc) // per_mod, 0, 5)),
                  pl.BlockSpec((1, d), lambda i, j: (0, 0)),
                  pl.BlockSpec((1, d), lambda i, j: (0, 0))],
        out_specs=pl.BlockSpec((1, tc, d), lambda i, j: (i, j, 0)),
        out_shape=jax.ShapeDtypeStruct((nb, nbt, d), F32),
        scratch_shapes=[pltpu.VMEM((tc, half), ys.dtype), pltpu.VMEM((tc, half), ys.dtype)],
        compiler_params=_cparams(("parallel", "arbitrary"), 56),
        name="moe_combine",
    )(pa, pb, ri, ys, x1, mod, lng, lnb)


def _tile_tables(cnt, tmx, max_tiles):
    nb = cnt.shape[0]
    nt = (cnt + (tmx - 1)) // tmx
    first_tile = jnp.cumsum(nt, axis=1) - nt
    nt_eb = nt.T.reshape(-1)
    ends = jnp.cumsum(nt_eb)
    total = ends[-1]
    t = jnp.arange(max_tiles, dtype=I32)
    valid = t < total
    tc = jnp.minimum(t, total - 1)
    seg = jnp.sum((ends[None, :] <= tc[:, None]).astype(I32), axis=1)
    within = tc - (ends[seg] - nt_eb[seg])
    te = seg // nb
    tb = seg % nb
    tr = first_tile[tb, te] + within
    prev = jnp.concatenate([jnp.full((1,), -1, I32), te[:-1]])
    tf = (valid & (te != prev)).astype(I32)
    return tb.astype(I32), tr.astype(I32), te.astype(I32), valid.astype(I32), tf


def _moe(h2p, ri, x1, mod, w1, w3, w2, lng, lnb, nbt, tmx, tc):
    b, s, d = x1.shape
    nb = (b * s) // nbt
    rp = 2 * nbt + N_EXPERTS * tmx
    max_tiles = nb * (2 * nbt // tmx + N_EXPERTS)
    r = ri.reshape(nb, nbt, LANES)
    e1 = r[:, :, 0].astype(I32).reshape(nb, 1, nbt)
    e2 = r[:, :, 1].astype(I32).reshape(nb, 1, nbt)
    pa, pb, cnt = _moe_sort(e1, e2, tmx)
    tables = _tile_tables(cnt[:, :, 0], tmx, max_tiles)
    xs = _moe_dispatch(pa, pb, h2p.reshape(nb, nbt, h2p.shape[-1]), rp)
    ys = _moe_experts(tables, xs, w1, w3, w2, tmx)
    x2 = _moe_combine(pa, pb, r, ys, x1.reshape(nb, nbt, d), mod, lng, lnb, tc)
    return x2.reshape(b, s, d)


def _moe_dense_body(h_ref, ri_ref, x1_ref, g2_ref, lng_ref, lnb_ref, w1_ref, w3_ref, w2_ref, o_ref, acc):
    e = pl.program_id(0)

    @pl.when(e == 0)
    def _():
        acc[...] = jnp.zeros_like(acc)

    x = h_ref[0]
    ri = ri_ref[0]
    ef = e.astype(F32)
    comb = jnp.where(ri[:, 0:1] == ef, ri[:, 2:3], 0.0) + jnp.where(ri[:, 1:2] == ef, ri[:, 3:4], 0.0)
    hact = _silu(_dot3(x, w1_ref[0])) * _dot3(x, w3_ref[0]) * comb
    acc[...] += _dot3(hact, w2_ref[0])

    @pl.when(e == pl.num_programs(0) - 1)
    def _():
        y = DN_ALPHA * x1_ref[0] + (1.0 + g2_ref[0]) * acc[...]
        o_ref[0] = _layer_norm(y, lng_ref[...], lnb_ref[...])


def _moe_dense(h2, ri, x1, mod, w1, w3, w2, lng, lnb):
    _, r, d = x1.shape
    ne, _, ff = w1.shape
    row = lambda n: pl.BlockSpec((1, r, n), lambda e: (0, 0, 0))
    return pl.pallas_call(
        _moe_dense_body,
        grid=(ne,),
        in_specs=[row(d), row(LANES), row(d),
                  pl.BlockSpec((1, r, d), lambda e: (0, 0, 5)),
                  pl.BlockSpec((1, d), lambda e: (0, 0)), pl.BlockSpec((1, d), lambda e: (0, 0)),
                  pl.BlockSpec((1, d, ff), lambda e: (e, 0, 0)), pl.BlockSpec((1, d, ff), lambda e: (e, 0, 0)),
                  pl.BlockSpec((1, ff, d), lambda e: (e, 0, 0))],
        out_specs=row(d),
        out_shape=jax.ShapeDtypeStruct((1, r, d), F32),
        scratch_shapes=[pltpu.VMEM((r, d), F32)],
        compiler_params=_cparams(("arbitrary",)),
        name="moe_dense",
    )(h2, ri, x1, mod, lng, lnb, w1, w3, w2)


def _lambda(lam_ref):
    lp = lam_ref[...]
    a = jnp.sum(lp[0:1, :] * lp[1:2, :], axis=1, keepdims=True)
    c = jnp.sum(lp[2:3, :] * lp[3:4, :], axis=1, keepdims=True)
    return jnp.exp(a) - jnp.exp(c)


POS_SPLIT = 16


ONES_ROWS = 16


def _attn_body(slopes_ref, q_ref, k_ref, v_ref, lam_ref, o_ref, kaug, vt, qaug, m_s, acc, sbuf, *, lam_init, vd):
    hg = pl.program_id(1)
    j = pl.program_id(2)
    tq = q_ref.shape[1]
    group = q_ref.shape[2] // vd
    s_len = k_ref.shape[1]
    tk = tq

    @pl.when(j == 0)
    def _():
        pos = lax.broadcasted_iota(I32, (s_len, LANES), 0)
        ln = lax.broadcasted_iota(I32, (s_len, LANES), 1)
        hi = (pos // POS_SPLIT).astype(F32)
        lo = (pos % POS_SPLIT).astype(F32)
        ktail = jnp.where(ln == 0, hi, jnp.where(ln == 1, lo, 0.0)).astype(BF16)
        orow = lax.broadcasted_iota(I32, (ONES_ROWS, s_len), 0)
        ones_rows = jnp.where(orow == 0, 1.0, 0.0).astype(BF16)
        for g in range(group):
            kaug[g, :, 0:vd] = k_ref[0, :, g * vd:(g + 1) * vd].astype(BF16)
            kaug[g, :, vd:vd + LANES] = ktail
            vt[g, 0:vd, :] = v_ref[0, :, g * vd:(g + 1) * vd].T.astype(BF16)
            vt[g, vd:vd + ONES_ROWS, :] = ones_rows

    lane = lax.broadcasted_iota(I32, (tq, vd), 1)
    for g in range(group):
        slope = slopes_ref[hg * group + g]
        q = q_ref[0, :, g * vd:(g + 1) * vd].astype(F32)
        extra = jnp.where(lane == 0, slope * POS_SPLIT, jnp.where(lane == 1, slope, 0.0)).astype(BF16)
        qaug[g, 0:tq, 0:vd] = jnp.where(lane < vd // 2, q, 0.0).astype(BF16)
        qaug[g, 0:tq, vd:vd + LANES] = extra
        qaug[g, tq:2 * tq, 0:vd] = jnp.where(lane >= vd // 2, q, 0.0).astype(BF16)
        qaug[g, tq:2 * tq, vd:vd + LANES] = extra
    m_s[...] = jnp.full_like(m_s, -jnp.inf)
    acc[...] = jnp.zeros_like(acc)

    def scores(kb):
        off = pl.multiple_of(kb * tk, tk)
        return [_dot_nt(kaug[g, pl.ds(off, tk), :], qaug[g]) for g in range(group)]

    def consume(kb, masked):
        off = pl.multiple_of(kb * tk, tk)
        for g in range(group):
            st = sbuf[g]
            if masked:
                r = lax.broadcasted_iota(I32, st.shape, 0)
                c = lax.broadcasted_iota(I32, st.shape, 1)
                st = jnp.where(r <= (c & (tq - 1)), st, -jnp.inf)
            m_old = m_s[g]
            m_new = jnp.maximum(m_old, jnp.max(st, axis=0, keepdims=True))
            a = jnp.exp(m_old - m_new)
            p = jnp.exp(st - m_new)
            acc[g] = a * acc[g] + _dot(vt[g, :, pl.ds(off, tk)], p.astype(BF16))
            m_s[g] = m_new

    def put(sts):
        for g in range(group):
            sbuf[g] = sts[g]

    def off_diagonal(kb, carry):
        nxt = scores(kb + 1)
        consume(kb, False)
        put(nxt)
        return carry

    put(scores(0))
    lax.fori_loop(0, j, off_diagonal, 0)
    consume(j, True)
    lam = _lambda(lam_ref) + lam_init
    for g in range(group):
        num = acc[g, 0:vd, :]
        den = acc[g, vd:vd + 1, :]
        ot = num[:, 0:tq] / den[:, 0:tq] - lam * (num[:, tq:2 * tq] / den[:, tq:2 * tq])
        o_ref[0, :, g * vd:(g + 1) * vd] = ot.T


def _attn_prompt(q, k, v, lam_p, slopes, lam_init, tq, group):
    b, s, d = q.shape
    heads = slopes.shape[0]
    vd = d // heads
    gw = group * vd
    assert heads % group == 0 and tq & (tq - 1) == 0
    assert s // POS_SPLIT <= 256
    grid_spec = pltpu.PrefetchScalarGridSpec(
        num_scalar_prefetch=1,
        grid=(b, heads // group, s // tq),
        in_specs=[pl.BlockSpec((1, tq, gw), lambda i, h, j, sl: (i, j, h)),
                  pl.BlockSpec((1, s, gw), lambda i, h, j, sl: (i, 0, h)),
                  pl.BlockSpec((1, s, gw), lambda i, h, j, sl: (i, 0, h)),
                  pl.BlockSpec(lam_p.shape, lambda i, h, j, sl: (0, 0))],
        out_specs=pl.BlockSpec((1, tq, gw), lambda i, h, j, sl: (i, j, h)),
        scratch_shapes=[pltpu.VMEM((group, s, vd + LANES), BF16), pltpu.VMEM((group, vd + ONES_ROWS, s), BF16),
                        pltpu.VMEM((group, 2 * tq, vd + LANES), BF16),
                        pltpu.VMEM((group, 1, 2 * tq), F32),
                        pltpu.VMEM((group, vd + ONES_ROWS, 2 * tq), F32),
                        pltpu.VMEM((group, tq, 2 * tq), F32)],
    )
    return pl.pallas_call(
        functools.partial(_attn_body, lam_init=lam_init, vd=vd),
        grid_spec=grid_spec,
        out_shape=jax.ShapeDtypeStruct((b, s, d), F32),
        compiler_params=_cparams(("parallel", "parallel", "arbitrary"), 56),
        name="attn_prompt",
    )(slopes, q, k, v, lam_p)


def _decode_body(pt_ref, q_ref, kn_ref, vn_ref, slope_ref, lam_ref, *rest, lam_init, heads):
    npg = PAGES_PER_STEP
    k_refs = rest[:npg]
    v_refs = rest[npg:2 * npg]
    o_ref, m_s, l_s, acc, mb = rest[2 * npg:]
    j = pl.program_id(1)
    nj = pl.num_programs(1)
    vd = q_ref.shape[3]
    rows = 2 * heads
    keys = PAGE_SIZE * heads
    step_keys = npg * PAGE_SIZE
    lane = lax.broadcasted_iota(I32, (heads, vd), 1)
    qf = q_ref[0, 0].astype(F32)
    qm = jnp.concatenate([jnp.where(lane < vd // 2, qf, 0.0), jnp.where(lane >= vd // 2, qf, 0.0)], axis=0)
    shift = slope_ref[...] * float(step_keys)

    @pl.when(j == 0)
    def _():
        kn = kn_ref[0, 0]
        vn = vn_ref[0, 0]
        s_self = jnp.sum(qm * jnp.concatenate([kn, kn], axis=0), axis=1, keepdims=True)
        m_s[...] = s_self + shift * (1.0 * nj)
        l_s[...] = jnp.ones_like(l_s)
        acc[...] = jnp.concatenate([vn, vn], axis=0)
        r_i = lax.broadcasted_iota(I32, mb.shape, 0)
        c_i = lax.broadcasted_iota(I32, mb.shape, 1)
        dist = (step_keys - c_i // heads).astype(F32)
        mb[...] = jnp.where((c_i % heads) == (r_i % heads), -slope_ref[...] * dist, -jnp.inf)

    qh, ql = _split(qm)
    qhl = jnp.concatenate([qh, ql], axis=0)

    def scores(i):
        both = _dot_nt(qhl, k_refs[i][0].reshape(keys, vd).astype(BF16))
        return both[0:rows] + both[rows:2 * rows]

    s = jnp.concatenate([scores(i) for i in range(npg)], axis=1) + mb[...]
    m_old = m_s[...] - shift
    m_new = jnp.maximum(m_old, jnp.max(s, axis=1, keepdims=True))
    a = jnp.exp(m_old - m_new)
    p = jnp.exp(s - m_new)
    l_s[...] = a * l_s[...] + jnp.sum(p, axis=1, keepdims=True)
    ph, plo = _split(p)
    phl = jnp.concatenate([ph, plo], axis=0)
    pv = jnp.zeros((rows, vd), F32)
    for i in range(npg):
        both = _dot(phl[:, i * keys:(i + 1) * keys], v_refs[i][0].reshape(keys, vd).astype(BF16))
        pv = pv + both[0:rows] + both[rows:2 * rows]
    acc[...] = a * acc[...] + pv
    m_s[...] = m_new

    @pl.when(j == nj - 1)
    def _():
        lam = _lambda(lam_ref) + lam_init
        o16 = acc[...] / l_s[...]
        o_ref[0, 0] = o16[0:heads, :] - lam * o16[heads:rows, :]


def _attn_decode(q, k_new, v_new, cache_k, cache_v, page_table, lam_p, slopes, lam_init):
    db, _, heads, vd = q.shape
    n_pages = page_table.shape[1]
    npg = PAGES_PER_STEP
    assert n_pages % npg == 0
    rows = 2 * heads
    slope16 = jnp.concatenate([slopes, slopes]).reshape(rows, 1)
    tok = pl.BlockSpec((1, 1, heads, vd), lambda i, j, pt: (i, 0, 0, 0))

    def page_spec(idx):
        return pl.BlockSpec((1, PAGE_SIZE, heads, vd), lambda i, j, pt: (pt[i, j * npg + idx], 0, 0, 0))

    grid_spec = pltpu.PrefetchScalarGridSpec(
        num_scalar_prefetch=1,
        grid=(db, n_pages // npg),
        in_specs=[tok, tok, tok,
                  pl.BlockSpec((rows, 1), lambda i, j, pt: (0, 0)),
                  pl.BlockSpec(lam_p.shape, lambda i, j, pt: (0, 0))]
                 + [page_spec(i) for i in range(npg)] + [page_spec(i) for i in range(npg)],
        out_specs=tok,
        scratch_shapes=[pltpu.VMEM((rows, 1), F32), pltpu.VMEM((rows, 1), F32), pltpu.VMEM((rows, vd), F32),
                        pltpu.VMEM((rows, npg * PAGE_SIZE * heads), F32)],
    )
    return pl.pallas_call(
        functools.partial(_decode_body, lam_init=lam_init, heads=heads),
        grid_spec=grid_spec,
        out_shape=jax.ShapeDtypeStruct((db, 1, heads, vd), F32),
        compiler_params=_cparams(("parallel", "arbitrary")),
        name="attn_decode",
    )(page_table, q, k_new, v_new, slope16, lam_p, *([cache_k] * npg), *([cache_v] * npg))


MXU_WIDTH = 256


def _headwise_blocks(w):
    rows = w.reshape(-1, MXU_WIDTH, A_QKV_BLOCK)
    tiled = jnp.tile(rows, (1, 1, MXU_WIDTH // A_QKV_BLOCK))
    r = lax.broadcasted_iota(I32, tiled.shape, 1) // A_QKV_BLOCK
    s = lax.broadcasted_iota(I32, tiled.shape, 2) // A_QKV_BLOCK
    return jnp.where(r == s, tiled, 0.0)


def _prep_mlstm(conv_w, conv_b, w_q, w_k, w_v, w_gate, b_gate, heads, dt):
    inner = conv_w.shape[1]
    hd = inner // heads
    ng = b_gate.shape[0]
    wg = w_gate.reshape(heads, 3, hd, ng).transpose(1, 0, 2, 3).reshape(3 * inner, ng)
    wgp = jnp.pad(wg, ((0, 0), (0, LANES - ng)))
    return dict(conv_w=conv_w, conv_b=conv_b.reshape(1, inner),
                wqkv=jnp.stack([_headwise_blocks(w_q), _headwise_blocks(w_k) * hd ** -0.5,
                                _headwise_blocks(w_v)]).astype(dt),
                wg=wgp.astype(dt), wgt=wg.T.astype(BF16),
                bg=jnp.pad(b_gate, (0, LANES - ng)).reshape(1, LANES), bgt=b_gate.reshape(ng, 1))


def _prep_router(w_group, b_group, w_expert, b_expert):
    d = w_group.shape[0]
    we = w_expert.transpose(1, 0, 2).reshape(d, N_EXPERTS)
    wr = jnp.pad(jnp.concatenate([w_group, we], axis=1), ((0, 0), (0, LANES - MOE_GROUPS - N_EXPERTS)))
    br = jnp.pad(jnp.concatenate([b_group, b_expert.reshape(-1)]), (0, LANES - MOE_GROUPS - N_EXPERTS))
    wrh = wr.astype(BF16)
    wrl = (wr - wrh.astype(F32)).astype(BF16)
    return wrh, wrl, br.reshape(1, LANES)


def _trunk(x, mod, conv0, state, cache, wts, cfg):
    b, s, d = x.shape
    heads_a = wts["heads_a"]
    ts = cfg["ts"]
    row_shape = cfg["row_shape"]
    xr = x.reshape(row_shape + (d,))

    m0 = mod[0]
    tn = cfg["tn"]
    xm, z = _linear(xr, wts["a_w_in"], ts, (F32, F32), mod=m0, cols=(1, 0), tn=tn, name="a_in")
    inner = xm.shape[-1]
    hd = inner // heads_a
    if cfg["chunk"] is not None:
        tail8 = jnp.pad(conv0, ((0, 0), (8 - (A_CONV - 1), 0), (0, 0)))
        xc, q, k, v, gt, conv8 = _mlstm_pre(xm, tail8, wts["a_prep"], cfg["ts_pre"])
        conv_new = conv8[:, 8 - (A_CONV - 1):, :]
        grow = gt.reshape(b, 2, heads_a, s).transpose(0, 2, 1, 3)
        hh, c_new, n_new, m_new = _mlstm_scan(q, k, v, grow, cfg["chunk"])
    else:
        xc, q, k, v, g = _mlstm_pre_rows(xm, conv0.transpose(1, 0, 2), wts["a_prep"])
        conv_new = jnp.concatenate([conv0[:, 1:, :], xm.reshape(b, s, inner)], axis=1)
        c0, n0, mm0 = state
        g4 = g.reshape(b, 2, heads_a).transpose(0, 2, 1).reshape(b, heads_a, 1, 2)
        tok = lambda a: a.reshape(b, s, inner)
        hh, c_new, n_new, m_new = _mlstm_step(tok(q), tok(k), tok(v), g4, c0, n0.reshape(b, heads_a, 1, hd),
                                              mm0.reshape(b, heads_a, 1, 1))
    n_new = n_new.reshape(b, heads_a, hd)
    m_new = m_new.reshape(b, heads_a)
    rs = lambda a: a.reshape(row_shape + (a.shape[-1],))
    x1, h2p, ri = _mixer_out(
        functools.partial(_mixa_body, heads=heads_a), [rs(hh), rs(xc), z], xr, m0,
        [wts["a_norm_w"], wts["a_skip"], wts["a_w_out"], wts["ln_g"][0][0], wts["ln_b"][0][0], *wts["router"][0]],
        cfg["ts_out"], cfg["packed"], "mix_a")

    def moe(h2p, ri, xres, m, layer):
        lng, lnb = wts["ln_g"][layer][1], wts["ln_b"][layer][1]
        if cfg["dense_moe"]:
            return _moe_dense(h2p, ri, xres, m, *wts["moe"][layer], lng, lnb)
        return _moe(h2p, ri, xres, m, *wts["moe"][layer], lng, lnb, cfg["nbt"], cfg["tmx"], cfg["tc"])

    x2 = moe(h2p, ri, x1, m0, 0)

    k_sh, v_sh = _linear(x2, wts["kv_w"], ts, (F32, F32), tn=tn, name="kv_proj")

    m1 = mod[1]
    lam_init = 0.8 - 0.6 * math.exp(-0.3 * 1)
    (qq,) = _linear(x2, wts["b_w_q"], ts, (cfg["q_dtype"],), mod=m1, cols=(1, 0), out_scale=wts["q_scale"],
                    tn=tn, name="q_proj")
    if cache is None:
        o = _attn_prompt(qq.reshape(b, s, d), k_sh.reshape(b, s, d), v_sh.reshape(b, s, d),
                         wts["b_lambda"], wts["slopes"], lam_init, cfg["tq"], cfg["head_group"])
    else:
        cache_k, cache_v, page_table = cache
        hv = cache_k.shape[2:]
        o = _attn_decode(qq.reshape((b, s) + hv), k_sh.reshape((b, s) + hv), v_sh.reshape((b, s) + hv),
                         cache_k, cache_v, page_table, wts["b_lambda"], wts["slopes"], lam_init).reshape(b, s, d)
    heads_b = wts["slopes"].shape[0]
    x3, h2p, ri = _mixer_out(
        functools.partial(_mixb_body, heads=heads_b, gain=1.0 - lam_init), [rs(o)], x2, m1,
        [wts["b_subln_w"], wts["b_w_out"], wts["ln_g"][1][0], wts["ln_b"][1][0], *wts["router"][1]],
        cfg["ts_out"], cfg["packed"], "mix_b")
    y = moe(h2p, ri, x3, m1, 1)
    return (y.reshape(b, s, d), k_sh.reshape(b, s, d), v_sh.reshape(b, s, d), conv_new, c_new, n_new, m_new)


def kernel(x_prompt, x_sample, cache_k, cache_v, state_conv, state_C, state_n, state_m, page_table, c_prompt, c_sample, a_w_in, a_conv_w, a_conv_b, a_w_q, a_w_k, a_w_v, a_w_gate, a_b_gate, a_norm_w, a_skip, a_w_out, kv_w_k, kv_w_v, b_w_q, b_lambda, b_subln_w, b_w_out, moe_w_group, moe_b_group, moe_w_expert, moe_b_expert, moe_w1, moe_w3, moe_w2, ln_g, ln_b, ada_w, ada_b):
    bp, sp, d = x_prompt.shape
    db, ds, _ = x_sample.shape
    assert ds == 1
    heads_a = state_C.shape[2]
    heads_b = cache_k.shape[2]
    inner = a_conv_w.shape[2]
    hd_b = d // (2 * heads_b)
    ff = moe_w1.shape[-1]

    slopes = jnp.exp2(-8.0 * jnp.arange(1, heads_b + 1, dtype=F32) / heads_b)
    common = dict(
        heads_a=heads_a,
        a_norm_w=a_norm_w[0].reshape(1, inner), a_skip=a_skip[0].reshape(1, inner),
        q_scale=hd_b ** -0.5, b_lambda=b_lambda[0], b_subln_w=b_subln_w[0].reshape(1, -1), slopes=slopes,
        router=[_prep_router(moe_w_group[i], moe_b_group[i], moe_w_expert[i], moe_b_expert[i]) for i in range(DEPTH)],
        moe=[(moe_w1[i].reshape(N_EXPERTS, d, ff), moe_w3[i].reshape(N_EXPERTS, d, ff),
              moe_w2[i].reshape(N_EXPERTS, ff, d)) for i in range(DEPTH)],
        ln_g=[[ln_g[i, j].reshape(1, d) for j in range(2)] for i in range(DEPTH)],
        ln_b=[[ln_b[i, j].reshape(1, d) for j in range(2)] for i in range(DEPTH)],
    )

    def weights(dt):
        return dict(
            common,
            a_w_in=a_w_in[0].astype(dt),
            a_prep=_prep_mlstm(a_conv_w[0], a_conv_b[0], a_w_q[0], a_w_k[0], a_w_v[0], a_w_gate[0], a_b_gate[0],
                               heads_a, dt),
            a_w_out=a_w_out[0].astype(dt),
            kv_w=jnp.concatenate([kv_w_k, kv_w_v], axis=1).astype(dt),
            b_w_q=b_w_q[0].astype(dt), b_w_out=b_w_out[0].astype(dt))

    mod = _ada(jnp.concatenate([c_prompt, c_sample], axis=0), ada_w, ada_b)
    mod_p = mod[:, :bp].reshape(DEPTH, bp, 1, 6 * d)
    mod_s = mod[:, bp:].reshape(DEPTH, 1, db, 6 * d)

    cfg_p = dict(ts=512, ts_pre=256, ts_out=256, chunk=256, row_shape=(bp, sp),
                 nbt=sp, tmx=256, tc=512, tq=512,
                 head_group=4, tn=None, packed=False, dense_moe=False, q_dtype=BF16)
    zero_conv = jnp.zeros((bp, A_CONV - 1, inner), F32)
    outs_p = _trunk(x_prompt, mod_p, zero_conv, None, None, weights(BF16), cfg_p)

    cfg_s = dict(ts=db, ts_pre=1, ts_out=db, chunk=None, row_shape=(1, db), tn=d, packed=False, dense_moe=True,
                 q_dtype=F32)
    cache = (cache_k, cache_v, page_table)
    outs_s = _trunk(x_sample, mod_s, state_conv[0], (state_C[0], state_n[0], state_m[0]), cache, weights(F32), cfg_s)

    def pack(o, nb_, s_):
        y, k, v, conv, c, n, m = o
        return (y, k.reshape(nb_, s_, heads_b, 2 * hd_b), v.reshape(nb_, s_, heads_b, 2 * hd_b),
                conv[None], c[None], n[None], m[None])

    yp, kp, vp, convp, cp, np_, mp = pack(outs_p, bp, sp)
    ys, ks, vs, convs, cs, ns, ms_ = pack(outs_s, db, ds)
    return (yp, ys, kp, vp, convp, cp, np_, mp, ks, vs, convs, cs, ns, ms_)
```
